```python
import math
import jax
import jax.numpy as jnp
from jax import lax
import numpy as np

D_MODEL = 1024
BATCH = 16
SEQ = 2048
DEPTH = 4
DEC_BATCH = 128
DEC_SEQ = 8
PAST_LEN = 8192
PAGE_SIZE = 128

A_HEADS = 4
A_KV_HEADS = 2
A_HEAD_DIM = 64
MOBA_BLOCK = 256
MOBA_TOPK = 3
B_HEADS = 4
B_KV_HEADS = 1
B_QK_DIM = 32
B_V_DIM = 2 * B_QK_DIM
B_ROW = 2 * B_QK_DIM + B_V_DIM
C_HEADS = 4
C_Q_LORA = 256
C_KV_LORA = 128
C_NOPE_DIM = 64
C_ROPE_DIM = 32
C_QK_DIM = C_NOPE_DIM + C_ROPE_DIM
C_V_DIM = 64
ROPE_THETA = 10000.0
POOL_WINDOWS = (2, 4, 8, 16)
POOL_GROUP = 64
D_POOL = len(POOL_WINDOWS) * POOL_GROUP
POOL_STATE = max(POOL_WINDOWS) - 1
N_BRANCHES = 4
BRANCH_WIDTH = 256
IN_SIZES = (A_HEADS * A_HEAD_DIM, A_KV_HEADS * A_HEAD_DIM, A_KV_HEADS * A_HEAD_DIM,
            B_HEADS * 2 * B_QK_DIM, B_KV_HEADS * 2 * B_QK_DIM, B_KV_HEADS * B_V_DIM,
            C_Q_LORA, C_KV_LORA + C_ROPE_DIM, D_POOL, N_BRANCHES * D_MODEL)
IN_WIDTH = sum(IN_SIZES)
D_FF = 2816
CONV_WIDTH = 3
Q_BLOCK = 128
EPS = 1e-6

kernel_name = 'hybrid_moba_diff_mla_pool_decoder_step'


def _rms(x, g):
    xf = x.astype(jnp.float32)
    y = xf * lax.rsqrt(jnp.mean(xf * xf, axis=-1, keepdims=True) + EPS)
    return (y * g.astype(jnp.float32)).astype(x.dtype)


def _alibi(n):
    return jnp.asarray(2.0 ** (-8.0 * np.arange(1, n + 1) / n), dtype=jnp.float32)


def _rope(x, pos):
    half = C_ROPE_DIM // 2
    inv = ROPE_THETA ** (-jnp.arange(half, dtype=jnp.float32) / half)
    ang = pos.astype(jnp.float32)[:, None] * inv[None, :]
    shape = (1, pos.shape[0]) + (1,) * (x.ndim - 3) + (half,)
    cos, sin = jnp.cos(ang).reshape(shape), jnp.sin(ang).reshape(shape)
    xf = x.astype(jnp.float32)
    x1, x2 = xf[..., :half], xf[..., half:]
    return jnp.concatenate([x1 * cos - x2 * sin, x2 * cos + x1 * sin], axis=-1).astype(x.dtype)


def _split_in(z):
    offs = np.cumsum(IN_SIZES)[:-1].tolist()
    return jnp.split(z, offs, axis=-1)


def _attn_probs(q, k, q_pos, k_pos, slopes):
    hk, g = q.shape[2], q.shape[3]
    s = jnp.einsum('bqkgd,bskd->bkgqs', q, k, preferred_element_type=jnp.float32) * (q.shape[-1] ** -0.5)
    dist = (q_pos[:, None] - k_pos[None, :]).astype(jnp.float32)
    if slopes is not None:
        s = s - slopes.reshape(1, hk, g, 1, 1) * dist
    s = jnp.where(dist >= 0, s, -jnp.inf)
    return jax.nn.softmax(s, axis=-1)


def _attn(q, k, v, q_pos, k_pos, slopes):
    b, tq, h, d = q.shape
    hk = k.shape[2]
    p = _attn_probs(q.reshape(b, tq, hk, h // hk, d), k, q_pos, k_pos, slopes)
    o = jnp.einsum('bkgqs,bskd->bqkgd', p.astype(v.dtype), v)
    return o.reshape(b, tq, h * v.shape[-1])


def _sweep(fn, q, q_pos):
    b, t = q.shape[:2]
    qb = min(t, Q_BLOCK)
    n = t // qb
    qs = jnp.moveaxis(q.reshape((b, n, qb) + q.shape[2:]), 1, 0)
    out = lax.map(lambda a: fn(a[0], a[1]), (qs, q_pos.reshape(n, qb)))
    return jnp.moveaxis(out, 0, 1).reshape(b, t, -1)


def _moba_seq(q, q_pos, kb, vb):
    nb = kb.shape[0]
    n_sel = min(MOBA_TOPK, nb)
    kv_of_h = jnp.arange(A_HEADS) // (A_HEADS // A_KV_HEADS)
    slopes = _alibi(A_HEADS)
    scale = A_HEAD_DIM ** -0.5
    k_mean = jnp.mean(kb.astype(jnp.float32), axis=1)[:, kv_of_h]
    kbt = jnp.moveaxis(kb, 2, 0)
    vbt = jnp.moveaxis(vb, 2, 0)
    r = jnp.arange(MOBA_BLOCK, dtype=jnp.int32)

    def chunk(qc, pc):
        nq = qc.shape[0]
        cur = pc // MOBA_BLOCK
        bscore = jnp.einsum('qhd,nhd->qhn', qc.astype(jnp.float32), k_mean)
        bscore = jnp.where(jnp.arange(nb)[None, None, :] < cur[:, None, None], bscore, -jnp.inf)
        _, sel = lax.top_k(bscore, n_sel)
        sel_ok = sel < cur[:, None, None]
        ks = kbt[kv_of_h[None, :, None], sel]
        vs = vbt[kv_of_h[None, :, None], sel]
        own = cur[0]
        ko = lax.dynamic_index_in_dim(kb, own, 0, keepdims=False)[:, kv_of_h]
        vo = lax.dynamic_index_in_dim(vb, own, 0, keepdims=False)[:, kv_of_h]
        dist_sel = (pc[:, None, None, None] - (sel[..., None] * MOBA_BLOCK + r)).astype(jnp.float32)
        dist_own = (pc[:, None] - (own * MOBA_BLOCK + r)[None, :]).astype(jnp.float32)[:, None, :]
        s_sel = (jnp.einsum('qhd,qhnrd->qhnr', qc, ks, preferred_element_type=jnp.float32) * scale
                 - slopes[None, :, None, None] * dist_sel)
        s_sel = jnp.where(sel_ok[..., None], s_sel, -jnp.inf)
        s_own = (jnp.einsum('qhd,rhd->qhr', qc, ko, preferred_element_type=jnp.float32) * scale
                 - slopes[None, :, None] * dist_own)
        s_own = jnp.where(dist_own >= 0, s_own, -jnp.inf)
        s = jnp.concatenate([s_sel.reshape(nq, A_HEADS, n_sel * MOBA_BLOCK), s_own], axis=-1)
        p = jax.nn.softmax(s, axis=-1).astype(vb.dtype)
        p_sel = p[..., :n_sel * MOBA_BLOCK].reshape(nq, A_HEADS, n_sel, MOBA_BLOCK)
        p_own = p[..., n_sel * MOBA_BLOCK:]
        return jnp.einsum('qhnr,qhnrd->qhd', p_sel, vs) + jnp.einsum('qhr,rhd->qhd', p_own, vo)

    t = q.shape[0]
    csz = min(t, Q_BLOCK)
    n = t // csz
    out = lax.map(lambda a: chunk(a[0], a[1]),
                  (q.reshape(n, csz, A_HEADS, A_HEAD_DIM), q_pos.reshape(n, csz)))
    return out.reshape(t, A_HEADS * A_HEAD_DIM)


def _diff_attn(q, k, v, q_pos, k_pos, lam, lam_init, g_sub):
    b, tq = q.shape[:2]
    qg = q.reshape(b, tq, B_KV_HEADS, B_HEADS // B_KV_HEADS, 2, B_QK_DIM)
    slopes = _alibi(B_HEADS)
    p1 = _attn_probs(qg[..., 0, :], k[..., 0, :], q_pos, k_pos, slopes)
    p2 = _attn_probs(qg[..., 1, :], k[..., 1, :], q_pos, k_pos, slopes)
    o = jnp.einsum('bkgqs,bskd->bqkgd', (p1 - lam * p2).astype(v.dtype), v)
    o = _rms(o.reshape(b, tq, B_HEADS, B_V_DIM), g_sub) * (1.0 - lam_init)
    return o.reshape(b, tq, B_HEADS * B_V_DIM)


def _mla_seq(q, c_all, q_pos, k_pos, w_uk, w_uv, g_k):
    s = c_all.shape[0]
    lat = c_all[:, :C_KV_LORA]
    k_nope = jnp.einsum('sc,chd->shd', lat, w_uk)
    k_rot = jnp.broadcast_to(c_all[:, None, C_KV_LORA:], (s, C_HEADS, C_ROPE_DIM))
    k = _rms(jnp.concatenate([k_nope, k_rot], axis=-1), g_k)[None]
    v = jnp.einsum('sc,chd->shd', lat, w_uv)[None]
    return _sweep(lambda qb, pb: _attn(qb, k, v, pb, k_pos, None), q[None], q_pos)[0]


def _pool_mixer(u, pos, prev, w, scale):
    bsz, t, _ = u.shape
    full = jnp.concatenate([prev, u], axis=1)
    cs = jnp.cumsum(full.astype(jnp.float32), axis=1)
    cs = jnp.concatenate([jnp.zeros((bsz, 1, D_POOL), jnp.float32), cs], axis=1)
    end = cs[:, POOL_STATE + 1:POOL_STATE + 1 + t]
    means = []
    for gi, win in enumerate(POOL_WINDOWS):
        c0, c1 = gi * POOL_GROUP, (gi + 1) * POOL_GROUP
        start = cs[:, POOL_STATE + 1 - win:POOL_STATE + 1 - win + t, c0:c1]
        cnt = jnp.minimum(win, pos + 1).astype(jnp.float32)[None, :, None]
        means.append((end[..., c0:c1] - start) / cnt)
    d = (jnp.concatenate(means, axis=-1) - u.astype(jnp.float32)).astype(u.dtype)
    y = jnp.einsum('btgc,gcd->btgd', d.reshape(bsz, t, len(POOL_WINDOWS), POOL_GROUP), w)
    return y.reshape(bsz, t, D_POOL) * scale, full[:, -POOL_STATE:]


def _layer(x, pos, a_past, b_past, c_past, pool_prev, conv_prev, lw, lam_init):
    f32 = jnp.float32
    bsz, t, _ = x.shape
    xn = _rms(x, lw['norm1_g'])
    aq, ak, av, bq, bk, bv, cq, ckv, du, gl = _split_in(xn @ lw['w_in'])

    aq = _rms(aq.reshape(bsz, t, A_HEADS, A_HEAD_DIM), lw['a_q_g'])
    ak = _rms(ak.reshape(bsz, t, A_KV_HEADS, A_HEAD_DIM), lw['a_k_g'])
    a_new = jnp.concatenate([ak, av.reshape(bsz, t, A_KV_HEADS, A_HEAD_DIM)], axis=-1)
    a_all = jnp.concatenate([a_past.astype(a_new.dtype), a_new], axis=1)
    n_keys = a_all.shape[1]
    nb = -(-n_keys // MOBA_BLOCK)
    a_all = jnp.pad(a_all, ((0, 0), (0, nb * MOBA_BLOCK - n_keys), (0, 0), (0, 0)))
    a_all = a_all.reshape(bsz, nb, MOBA_BLOCK, A_KV_HEADS, 2 * A_HEAD_DIM)
    out_a = lax.map(lambda s: _moba_seq(s[0], pos, s[1][..., :A_HEAD_DIM], s[1][..., A_HEAD_DIM:]),
                    (aq, a_all))
    k_pos = jnp.arange(n_keys, dtype=jnp.int32)

    bq = _rms(bq.reshape(bsz, t, B_HEADS, 2, B_QK_DIM), lw['b_q_g'])
    bk = _rms(bk.reshape(bsz, t, B_KV_HEADS, 2, B_QK_DIM), lw['b_k_g'])
    b_new = jnp.concatenate([bk.reshape(bsz, t, B_KV_HEADS, 2 * B_QK_DIM),
                             bv.reshape(bsz, t, B_KV_HEADS, B_V_DIM)], axis=-1)
    b_all = jnp.concatenate([b_past.astype(b_new.dtype), b_new], axis=1)
    bk_all = b_all[..., :2 * B_QK_DIM].reshape(bsz, n_keys, B_KV_HEADS, 2, B_QK_DIM)
    bv_all = b_all[..., 2 * B_QK_DIM:]
    lam_p = lw['b_lam'].astype(f32)
    lam = jnp.exp(jnp.sum(lam_p[0] * lam_p[1])) - jnp.exp(jnp.sum(lam_p[2] * lam_p[3])) + lam_init
    out_b = _sweep(lambda qb_, pb_: _diff_attn(qb_, bk_all, bv_all, pb_, k_pos, lam, lam_init, lw['b_sub_g']),
                   bq, pos)

    cq = _rms(cq, lw['c_qa_g'])
    qc = jnp.einsum('btc,chd->bthd', cq, lw['w_uq'])
    qc = _rms(jnp.concatenate([qc[..., :C_NOPE_DIM], _rope(qc[..., C_NOPE_DIM:], pos)], axis=-1), lw['c_q_g'])
    c_new = jnp.concatenate([_rms(ckv[..., :C_KV_LORA], lw['c_kva_g']), _rope(ckv[..., C_KV_LORA:], pos)], axis=-1)
    c_all = jnp.concatenate([c_past.astype(c_new.dtype), c_new], axis=1)
    out_c = lax.map(lambda s: _mla_seq(s[0], s[1], pos, k_pos, lw['w_uk'], lw['w_uv'], lw['c_k_g']),
                    (qc, c_all))

    out_d, pool_new = _pool_mixer(du, pos, pool_prev.astype(du.dtype), lw['pool_w'], lw['pool_scale'])

    br = jnp.stack([out_a, out_b, out_c, out_d.astype(out_a.dtype)], axis=2)
    proj = jnp.einsum('btnc,ncd->btnd', br, lw['w_branch'], preferred_element_type=f32)
    gate = jax.nn.sigmoid(gl.reshape(bsz, t, N_BRANCHES, D_MODEL).astype(f32))
    merged = jnp.sum(gate * proj, axis=2).astype(x.dtype)
    h = x + merged @ lw['w_out']

    hn = _rms(h, lw['norm2_g'])
    up = hn @ lw['w_up']
    ga, va = up[..., :D_FF], up[..., D_FF:]
    ga_full = jnp.concatenate([conv_prev.astype(ga.dtype), ga], axis=1)
    conv = lw['conv_b'] + ga_full[:, 0:t] * lw['conv_w'][0]
    for i in range(1, CONV_WIDTH):
        conv = conv + ga_full[:, i:i + t] * lw['conv_w'][i]
    y = (jax.nn.silu(conv) * va) @ lw['w_down']
    return (h + y, a_new, b_new, c_new, pool_new, ga_full[:, -(CONV_WIDTH - 1):])


def setup_inputs(seed: int = 0) -> dict:
    key = jax.random.key(seed)
    ks = list(jax.random.split(key, 40))
    f32 = jnp.float32
    cnt = [0]

    def nk():
        cnt[0] += 1
        return ks[cnt[0] - 1]

    def nrm(shape, scale):
        return jax.random.normal(nk(), shape, f32) * scale

    def gain(shape):
        return 1.0 + 0.05 * jax.random.normal(nk(), shape, f32)

    n_pages = PAST_LEN // PAGE_SIZE
    n_used = DEC_BATCH * n_pages
    n_pool = n_used + max(1, n_used // 4)
    perm = jax.random.permutation(nk(), n_pool)
    page_table = perm[:n_used].reshape(DEC_BATCH, n_pages).astype(jnp.int32)
    return {
        'x_prompt': nrm((BATCH, SEQ, D_MODEL), 1.0),
        'x_sample': nrm((DEC_BATCH, DEC_SEQ, D_MODEL), 1.0),
        'cache_a': nrm((DEPTH, n_pool, PAGE_SIZE, A_KV_HEADS, 2 * A_HEAD_DIM), 1.0),
        'cache_b': nrm((DEPTH, n_pool, PAGE_SIZE, B_KV_HEADS, B_ROW), 1.0),
        'cache_c': nrm((DEPTH, n_pool, PAGE_SIZE, C_KV_LORA + C_ROPE_DIM), 1.0),
        'state_pool': nrm((DEPTH, DEC_BATCH, POOL_STATE, D_POOL), 1.0),
        'state_conv': nrm((DEPTH, DEC_BATCH, CONV_WIDTH - 1, D_FF), 1.0),
        'page_table': page_table,
        'norm1_g': gain((DEPTH, D_MODEL)),
        'w_in': nrm((DEPTH, D_MODEL, IN_WIDTH), D_MODEL ** -0.5),
        'a_q_g': gain((DEPTH, A_HEAD_DIM)),
        'a_k_g': gain((DEPTH, A_HEAD_DIM)),
        'b_q_g': gain((DEPTH, B_QK_DIM)),
        'b_k_g': gain((DEPTH, B_QK_DIM)),
        'b_lam': nrm((DEPTH, 4, B_QK_DIM), 0.1),
        'b_sub_g': gain((DEPTH, B_V_DIM)),
        'c_qa_g': gain((DEPTH, C_Q_LORA)),
        'c_kva_g': gain((DEPTH, C_KV_LORA)),
        'w_uq': nrm((DEPTH, C_Q_LORA, C_HEADS, C_QK_DIM), C_Q_LORA ** -0.5),
        'w_uk': nrm((DEPTH, C_KV_LORA, C_HEADS, C_NOPE_DIM), C_KV_LORA ** -0.5),
        'w_uv': nrm((DEPTH, C_KV_LORA, C_HEADS, C_V_DIM), C_KV_LORA ** -0.5),
        'c_q_g': gain((DEPTH, C_QK_DIM)),
        'c_k_g': gain((DEPTH, C_QK_DIM)),
        'pool_w': nrm((DEPTH, len(POOL_WINDOWS), POOL_GROUP, POOL_GROUP), POOL_GROUP ** -0.5),
        'pool_scale': gain((DEPTH, D_POOL)),
        'w_branch': nrm((DEPTH, N_BRANCHES, BRANCH_WIDTH, D_MODEL), BRANCH_WIDTH ** -0.5),
        'w_out': nrm((DEPTH, D_MODEL, D_MODEL), D_MODEL ** -0.5),
        'norm2_g': gain((DEPTH, D_MODEL)),
        'w_up': nrm((DEPTH, D_MODEL, 2 * D_FF), D_MODEL ** -0.5),
        'conv_w': nrm((DEPTH, CONV_WIDTH, D_FF), CONV_WIDTH ** -0.5),
        'conv_b': nrm((DEPTH, D_FF), 0.02),
        'w_down': nrm((DEPTH, D_FF, D_MODEL), D_FF ** -0.5),
    }


def reference(x_prompt, x_sample, cache_a, cache_b, cache_c, state_pool, state_conv, page_table,
              norm1_g, w_in, a_q_g, a_k_g, b_q_g, b_k_g, b_lam, b_sub_g, c_qa_g, c_kva_g,
              w_uq, w_uk, w_uv, c_q_g, c_k_g, pool_w, pool_scale, w_branch, w_out,
              norm2_g, w_up, conv_w, conv_b, w_down):
    bp, tp, _ = x_prompt.shape
    bs, ts, _ = x_sample.shape
    past_len = page_table.shape[1] * PAGE_SIZE
    pos_p = jnp.arange(tp, dtype=jnp.int32)
    pos_s = past_len + jnp.arange(ts, dtype=jnp.int32)
    dt = x_prompt.dtype
    empty_a = jnp.zeros((bp, 0, A_KV_HEADS, 2 * A_HEAD_DIM), dt)
    empty_b = jnp.zeros((bp, 0, B_KV_HEADS, B_ROW), dt)
    empty_c = jnp.zeros((bp, 0, C_KV_LORA + C_ROPE_DIM), dt)
    zero_pool = jnp.zeros((bp, POOL_STATE, D_POOL), dt)
    zero_conv = jnp.zeros((bp, CONV_WIDTH - 1, D_FF), dt)

    def paged(pool):
        rows = pool[page_table]
        return rows.reshape((bs, past_len) + pool.shape[2:])

    yp, ys = x_prompt, x_sample
    new_p = [[], [], [], [], []]
    new_s = [[], [], [], [], []]
    for l in range(DEPTH):
        lw = {'norm1_g': norm1_g[l], 'w_in': w_in[l], 'a_q_g': a_q_g[l], 'a_k_g': a_k_g[l],
              'b_q_g': b_q_g[l], 'b_k_g': b_k_g[l], 'b_lam': b_lam[l], 'b_sub_g': b_sub_g[l],
              'c_qa_g': c_qa_g[l], 'c_kva_g': c_kva_g[l], 'w_uq': w_uq[l], 'w_uk': w_uk[l],
              'w_uv': w_uv[l], 'c_q_g': c_q_g[l], 'c_k_g': c_k_g[l], 'pool_w': pool_w[l],
              'pool_scale': pool_scale[l], 'w_branch': w_branch[l], 'w_out': w_out[l],
              'norm2_g': norm2_g[l], 'w_up': w_up[l], 'conv_w': conv_w[l], 'conv_b': conv_b[l],
              'w_down': w_down[l]}
        lam_init = 0.8 - 0.6 * math.exp(-0.3 * l)
        outp = _layer(yp, pos_p, empty_a, empty_b, empty_c, zero_pool, zero_conv, lw, lam_init)
        outs = _layer(ys, pos_s, paged(cache_a[l]), paged(cache_b[l]), paged(cache_c[l]),
                      state_pool[l], state_conv[l], lw, lam_init)
        yp, ys = outp[0], outs[0]
        for i in range(5):
            new_p[i].append(outp[i + 1])
            new_s[i].append(outs[i + 1])
    np_ = [jnp.stack(v, axis=0) for v in new_p]
    ns_ = [jnp.stack(v, axis=0) for v in new_s]
    return (yp, ys, np_[0], np_[1], np_[2], np_[3], np_[4], ns_[0], ns_[1], ns_[2], ns_[3], ns_[4])
```

```python
import functools
import math

import numpy as np
import jax
import jax.numpy as jnp
from jax import lax
from jax.experimental import pallas as pl
from jax.experimental.pallas import tpu as pltpu

F32 = jnp.float32
BF16 = jnp.bfloat16

D_MODEL = 1024
PAGE = 128
A_HEADS, A_KV, A_DIM = 4, 2, 64
MOBA_BLOCK, MOBA_TOPK = 256, 3
B_HEADS, B_QK, B_V = 4, 32, 64
C_HEADS, C_QL, C_KVL, C_NOPE, C_ROPE, C_V = 4, 256, 128, 64, 32, 64
C_QK = C_NOPE + C_ROPE
CP = 128
CW = C_HEADS * CP
ROPE_THETA = 10000.0
POOL_WINDOWS = (2, 4, 8, 16)
POOL_GROUP = 64
D_POOL = 256
POOL_STATE = 15
D_FF = 2816
EPS = 1e-6
NEG = -1e30

ZS = 1664
O_AQ, O_AKV, O_BQ, O_CQ, O_DU, O_CKV, O_BKV = 0, 256, 512, 768, 1024, 1280, 1536

QB = 256
VMEM_LIMIT = 56 * 1024 * 1024


def _cparams(*sem):
    return pltpu.CompilerParams(dimension_semantics=sem, vmem_limit_bytes=VMEM_LIMIT)


def _resident(shape):
    nd = len(shape)
    return pl.BlockSpec(shape, lambda *_: (0,) * nd, pipeline_mode=pl.Buffered(1))


def _dot(a, b):
    return jnp.dot(a, b, preferred_element_type=F32)


def _dot_nt(a, b):
    return lax.dot_general(a, b, (((1,), (1,)), ((), ())), preferred_element_type=F32)


def _split16(x):
    hi = x.astype(BF16)
    lo = (x - hi.astype(F32)).astype(BF16)
    return hi, lo


def _inproj_body(x_ref, g_ref, ws_ref, wg_ref, zs_ref, gl_ref):
    x = x_ref[...]
    ms = jnp.mean(x * x, axis=-1, keepdims=True)
    xn = (x * lax.rsqrt(ms + EPS) * g_ref[...]).astype(BF16)
    zs_ref[...] = _dot(xn, ws_ref[...])
    for c in range(4):
        sl = slice(c * D_MODEL, (c + 1) * D_MODEL)
        gl_ref[:, sl] = _dot(xn, wg_ref[:, sl])


def _inproj(x, g, ws, wg, tm):
    n = x.shape[0]
    return pl.pallas_call(
        _inproj_body,
        grid=(n // tm,),
        in_specs=[pl.BlockSpec((tm, D_MODEL), lambda i: (i, 0)),
                  _resident((1, D_MODEL)), _resident(ws.shape), _resident(wg.shape)],
        out_specs=[pl.BlockSpec((tm, ZS), lambda i: (i, 0)),
                   pl.BlockSpec((tm, 4 * D_MODEL), lambda i: (i, 0))],
        out_shape=[jax.ShapeDtypeStruct((n, ZS), F32),
                   jax.ShapeDtypeStruct((n, 4 * D_MODEL), F32)],
        compiler_params=_cparams("parallel"),
        name="inproj",
    )(x, g, ws, wg)


G_AQ, G_AK, M_AK, G_BQ, G_BK, M_BK, G_CQA, G_CQ, G_CKVA, G_CK = range(10)


def _gnorm(x, gmat, n):
    ss = _dot((x * x).astype(BF16), gmat)
    return x * lax.rsqrt(ss * (1.0 / n) + EPS)


def _rope_cols(x, cos, sina, sinb):
    outs = []
    for c in range(x.shape[1] // 128):
        sl = slice(c * 128, (c + 1) * 128)
        xc = x[:, sl]
        outs.append(xc * cos[:, sl] + pltpu.roll(xc, 112, 1) * sina[:, sl]
                    + pltpu.roll(xc, 16, 1) * sinb[:, sl])
    return outs[0] if len(outs) == 1 else jnp.concatenate(outs, axis=-1)


def _prep_body(zs_ref, rt_ref, gn_ref, g64_ref, g32_ref, g96_ref, wuq_ref, wkc_ref, wuv_ref,
               qa_ref, an_ref, qb_ref, bn_ref, qc_ref, qce_ref, cn_ref, kc_ref, vc_ref):
    def gain(row, w):
        return gn_ref[row:row + 1, :w]

    aq = zs_ref[:, O_AQ:O_AQ + 256]
    qa_ref[...] = (_gnorm(aq, g64_ref[...], A_DIM) * gain(G_AQ, 256)).astype(BF16)
    akv = zs_ref[:, O_AKV:O_AKV + 256]
    an_ref[...] = jnp.where(gain(M_AK, 256) > 0.5,
                            _gnorm(akv, g64_ref[...], A_DIM) * gain(G_AK, 256), akv)
    bq = zs_ref[:, O_BQ:O_BQ + 256]
    qb_ref[...] = (_gnorm(bq, g32_ref[...], B_QK) * gain(G_BQ, 256)).astype(BF16)
    bkv = zs_ref[:, O_BKV:O_BKV + 128]
    bn_ref[...] = jnp.where(gain(M_BK, 128) > 0.5,
                            _gnorm(bkv, g32_ref[:128, :128], B_QK) * gain(G_BK, 128), bkv)
    cq = zs_ref[:, O_CQ:O_CQ + 256]
    cqn = cq * lax.rsqrt(jnp.mean(cq * cq, axis=-1, keepdims=True) + EPS) * gain(G_CQA, 256)
    qc = _dot(cqn.astype(BF16), wuq_ref[...])
    qc = _rope_cols(qc, rt_ref[:, 0:CW], rt_ref[:, CW:2 * CW], rt_ref[:, 2 * CW:3 * CW])
    qcn = _gnorm(qc, g96_ref[...], C_QK) * gain(G_CQ, CW)
    qc_ref[...] = qcn.astype(BF16)
    qce_ref[...] = (qcn * gain(G_CK, CW)).astype(BF16)
    lat = zs_ref[:, O_CKV:O_CKV + 128]
    latn = lat * lax.rsqrt(jnp.mean(lat * lat, axis=-1, keepdims=True) + EPS) * gain(G_CKVA, 128)
    kr = _rope_cols(zs_ref[:, O_CKV + 128:O_CKV + 256], rt_ref[:, 3 * CW:3 * CW + 128],
                    rt_ref[:, 3 * CW + 128:3 * CW + 256], rt_ref[:, 3 * CW + 256:3 * CW + 384])
    cn_ref[:, 0:128] = latn
    cn_ref[:, 128:160] = kr[:, 0:32]
    cfull = jnp.concatenate([latn, kr], axis=-1).astype(BF16)
    kc = _gnorm(_dot(cfull, wkc_ref[...]), g96_ref[...], C_QK) * gain(G_CK, CW)
    kc_ref[...] = kc.astype(BF16)
    vc_ref[...] = _dot(latn.astype(BF16), wuv_ref[...]).astype(BF16)


def _prep(zs, rt, gains, g64, g32, g96, wuq, wkc, wuv, tm):
    n = zs.shape[0]
    nt = rt.shape[0] // tm
    outs = [(256, BF16), (256, F32), (256, BF16), (128, F32), (CW, BF16), (CW, BF16),
            (160, F32), (CW, BF16), (256, BF16)]
    return pl.pallas_call(
        _prep_body,
        grid=(n // tm,),
        in_specs=[pl.BlockSpec((tm, ZS), lambda i: (i, 0)),
                  pl.BlockSpec((tm, rt.shape[1]), lambda i: (i % nt, 0)),
                  _resident(gains.shape), _resident(g64.shape), _resident(g32.shape),
                  _resident(g96.shape), _resident(wuq.shape), _resident(wkc.shape),
                  _resident(wuv.shape)],
        out_specs=[pl.BlockSpec((tm, w), lambda i: (i, 0)) for w, _ in outs],
        out_shape=[jax.ShapeDtypeStruct((n, w), dt) for w, dt in outs],
        compiler_params=_cparams("parallel"),
        name="prep",
    )(zs, rt, gains, g64, g32, g96, wuq, wkc, wuv)


def _flash_first(s, v, m_ref, l_ref, acc_ref, idx):
    m = jnp.max(s, axis=-1, keepdims=True)
    p = jnp.exp(s - m)
    m_ref[idx] = m
    l_ref[idx] = jnp.sum(p, axis=-1, keepdims=True)
    acc_ref[idx] = _dot(p.astype(BF16), v)


def _flash_next(s, v, m_ref, l_ref, acc_ref, idx):
    m_prev = m_ref[idx]
    m_new = jnp.maximum(m_prev, jnp.max(s, axis=-1, keepdims=True))
    alpha = jnp.exp(m_prev - m_new)
    p = jnp.exp(s - m_new)
    m_ref[idx] = m_new
    l_ref[idx] = alpha * l_ref[idx] + jnp.sum(p, axis=-1, keepdims=True)
    acc_ref[idx] = alpha * acc_ref[idx] + _dot(p.astype(BF16), v)


def _moba_prompt_body(q_ref, kv_ref, ald_ref, al0_ref, sc_ref, avg_ref, o_ref,
                      k16, v16, km_ref, sel_ref, m_ref, l_ref, acc_ref, *, nb):
    c = pl.program_id(1)

    @pl.when(c == 0)
    def _():
        kv = kv_ref[0]
        for j in range(A_KV):
            k16[j] = kv[:, j * 128:j * 128 + 64].astype(BF16)
            v16[j] = kv[:, j * 128 + 64:(j + 1) * 128].astype(BF16)
        hi, lo = _split16(kv.T)
        kmr = _dot(hi, avg_ref[...]) + _dot(lo, avg_ref[...])
        lane = lax.broadcasted_iota(jnp.int32, (64, 128), 1)
        rows = []
        for h in range(A_HEADS):
            j = h // (A_HEADS // A_KV)
            rows.append(jnp.where(lane // 32 == h, kmr[j * 128:j * 128 + 64, :], 0.0))
        km = jnp.concatenate(rows, axis=0)
        hi, lo = _split16(km)
        km_ref[0] = hi
        km_ref[1] = lo

    q = q_ref[...]
    bs = _dot(q, km_ref[0]) + _dot(q, km_ref[1])
    lane = lax.broadcasted_iota(jnp.int32, (QB, 128), 1)
    valid = (lane % 32) < c
    bsm = jnp.where(valid, bs, NEG)
    rank = jnp.zeros((QB, 128), F32)
    for k in range(1, nb):
        rank = rank + jnp.where(pltpu.roll(bsm, k, 1) >= bsm, 1.0, 0.0)
        rank = rank + jnp.where(pltpu.roll(bsm, 128 - k, 1) > bsm, 1.0, 0.0)
    sel_ref[...] = jnp.where(valid, jnp.where(rank < MOBA_TOPK, 1.0, 0.0), 0.0)

    start = pl.multiple_of(c * QB, QB)
    hpk = A_HEADS // A_KV
    for j in range(A_KV):
        qs = jnp.concatenate([q[:, (hpk * j + e) * 64:(hpk * j + e + 1) * 64] for e in range(hpk)],
                             axis=0)
        s = _dot_nt(qs, k16[j, pl.ds(start, QB), :]) + ald_ref[j]
        _flash_first(s, v16[j, pl.ds(start, QB), :], m_ref, l_ref, acc_ref, j)

        def body(n, carry, j=j, qs=qs):
            off = pl.multiple_of(n * QB, QB)
            far = ((c - n) * QB).astype(F32)
            s = _dot_nt(qs, k16[j, pl.ds(off, QB), :]) + al0_ref[j] - sc_ref[j] * far
            sel = sel_ref[...]
            cols = [jnp.max(jnp.where(lane == (hpk * j + e) * 32 + n, sel, 0.0),
                            axis=-1, keepdims=True) for e in range(hpk)]
            selcol = jnp.concatenate(cols, axis=0)
            s = jnp.where(selcol > 0.5, s, NEG)
            _flash_next(s, v16[j, pl.ds(off, QB), :], m_ref, l_ref, acc_ref, j)
            return carry

        lax.fori_loop(0, c, body, 0)
        o = acc_ref[j] / l_ref[j]
        for e in range(hpk):
            h = hpk * j + e
            o_ref[:, h * 64:(h + 1) * 64] = o[e * QB:(e + 1) * QB].astype(BF16)


def _moba_prompt(qa, a_new, consts, bsz, t):
    nq = t // QB
    nb = t // MOBA_BLOCK
    ald, al0, sc, avg = consts["a_ald"], consts["a_al0"], consts["a_sc"], consts["a_avg"]
    return pl.pallas_call(
        functools.partial(_moba_prompt_body, nb=nb),
        grid=(bsz, nq),
        in_specs=[pl.BlockSpec((QB, 256), lambda b, c: (b * nq + c, 0)),
                  pl.BlockSpec((1, t, 256), lambda b, c: (b, 0, 0)),
                  _resident(ald.shape), _resident(al0.shape), _resident(sc.shape),
                  _resident(avg.shape)],
        out_specs=pl.BlockSpec((QB, 256), lambda b, c: (b * nq + c, 0)),
        out_shape=jax.ShapeDtypeStruct((bsz * t, 256), BF16),
        scratch_shapes=[pltpu.VMEM((A_KV, t, 64), BF16), pltpu.VMEM((A_KV, t, 64), BF16),
                        pltpu.VMEM((2, 256, 128), BF16), pltpu.VMEM((QB, 128), F32),
                        pltpu.VMEM((A_KV, 2 * QB, 1), F32), pltpu.VMEM((A_KV, 2 * QB, 1), F32),
                        pltpu.VMEM((A_KV, 2 * QB, 64), F32)],
        compiler_params=_cparams("parallel", "arbitrary"),
        name="moba_prompt",
    )(qa, a_new.reshape(bsz, t, 256), ald, al0, sc, avg)


def _lam_of(lp, lam_init):
    a = jnp.sum(lp[0:1] * lp[1:2], axis=-1, keepdims=True)
    b = jnp.sum(lp[2:3] * lp[3:4], axis=-1, keepdims=True)
    return jnp.exp(a) - jnp.exp(b) + lam_init


def _diff_prompt_body(q_ref, kv_ref, lam_ref, gs_ref, ald_ref, al0_ref, sc_ref, o_ref,
                      k16, v16, m_ref, l_ref, acc_ref, *, lam_init):
    c = pl.program_id(1)

    @pl.when(c == 0)
    def _():
        kv = kv_ref[0]
        for i in range(2):
            k16[i] = kv[:, i * 32:(i + 1) * 32].astype(BF16)
        v16[...] = kv[:, 64:128].astype(BF16)

    q = q_ref[...]
    start = pl.multiple_of(c * QB, QB)
    for i in range(2):
        qs = jnp.concatenate([q[:, h * 64 + i * 32:h * 64 + (i + 1) * 32] for h in range(B_HEADS)],
                             axis=0)
        s = _dot_nt(qs, k16[i, pl.ds(start, QB), :]) + ald_ref[...]
        _flash_first(s, v16[pl.ds(start, QB), :], m_ref, l_ref, acc_ref, i)

        def body(n, carry, i=i, qs=qs):
            off = pl.multiple_of(n * QB, QB)
            far = ((c - n) * QB).astype(F32)
            s = _dot_nt(qs, k16[i, pl.ds(off, QB), :]) + al0_ref[...] - sc_ref[...] * far
            _flash_next(s, v16[pl.ds(off, QB), :], m_ref, l_ref, acc_ref, i)
            return carry

        lax.fori_loop(0, c, body, 0)

    lam = _lam_of(lam_ref[...], lam_init)
    o = acc_ref[0] / l_ref[0] - lam * (acc_ref[1] / l_ref[1])
    o = o * lax.rsqrt(jnp.mean(o * o, axis=-1, keepdims=True) + EPS) * gs_ref[...] * (1.0 - lam_init)
    for h in range(B_HEADS):
        o_ref[:, h * 64:(h + 1) * 64] = o[h * QB:(h + 1) * QB].astype(BF16)


def _diff_prompt(qb, b_new, b_lam, g_sub, consts, bsz, t, lam_init):
    nq = t // QB
    ald, al0, sc = consts["b_ald"], consts["b_al0"], consts["b_sc"]
    r = B_HEADS * QB
    return pl.pallas_call(
        functools.partial(_diff_prompt_body, lam_init=lam_init),
        grid=(bsz, nq),
        in_specs=[pl.BlockSpec((QB, 256), lambda b, c: (b * nq + c, 0)),
                  pl.BlockSpec((1, t, 128), lambda b, c: (b, 0, 0)),
                  _resident(b_lam.shape), _resident(g_sub.shape),
                  _resident(ald.shape), _resident(al0.shape), _resident(sc.shape)],
        out_specs=pl.BlockSpec((QB, 256), lambda b, c: (b * nq + c, 0)),
        out_shape=jax.ShapeDtypeStruct((bsz * t, 256), BF16),
        scratch_shapes=[pltpu.VMEM((2, t, 32), BF16), pltpu.VMEM((t, 64), BF16),
                        pltpu.VMEM((2, r, 1), F32), pltpu.VMEM((2, r, 1), F32),
                        pltpu.VMEM((2, r, 64), F32)],
        compiler_params=_cparams("parallel", "arbitrary"),
        name="diff_prompt",
    )(qb, b_new.reshape(bsz, t, 128), b_lam, g_sub, ald, al0, sc)


def _mla_prompt_body(q_ref, k_ref, v_ref, cm_ref, o_ref, v16, m_ref, l_ref, acc_ref):
    c = pl.program_id(1)

    @pl.when(c == 0)
    def _():
        for h in range(C_HEADS):
            v16[h] = v_ref[0, :, h * C_V:(h + 1) * C_V]

    start = pl.multiple_of(c * QB, QB)
    for h in range(C_HEADS):
        hs = slice(h * CP, (h + 1) * CP)
        qs = q_ref[:, hs]
        s = _dot_nt(qs, k_ref[0, pl.ds(start, QB), hs]) + cm_ref[...]
        _flash_first(s, v16[h, pl.ds(start, QB), :], m_ref, l_ref, acc_ref, h)

        def body(n, carry, h=h, hs=hs, qs=qs):
            off = pl.multiple_of(n * QB, QB)
            s = _dot_nt(qs, k_ref[0, pl.ds(off, QB), hs])
            _flash_next(s, v16[h, pl.ds(off, QB), :], m_ref, l_ref, acc_ref, h)
            return carry

        lax.fori_loop(0, c, body, 0)
        o_ref[:, h * C_V:(h + 1) * C_V] = (acc_ref[h] / l_ref[h]).astype(BF16)


def _mla_prompt(qc, kc, vc, consts, bsz, t):
    nq = t // QB
    cm = consts["c_cm"]
    return pl.pallas_call(
        _mla_prompt_body,
        grid=(bsz, nq),
        in_specs=[pl.BlockSpec((QB, CW), lambda b, c: (b * nq + c, 0)),
                  pl.BlockSpec((1, t, CW), lambda b, c: (b, 0, 0)),
                  pl.BlockSpec((1, t, 256), lambda b, c: (b, 0, 0)),
                  _resident(cm.shape)],
        out_specs=pl.BlockSpec((QB, 256), lambda b, c: (b * nq + c, 0)),
        out_shape=jax.ShapeDtypeStruct((bsz * t, 256), BF16),
        scratch_shapes=[pltpu.VMEM((C_HEADS, t, C_V), BF16),
                        pltpu.VMEM((C_HEADS, QB, 1), F32), pltpu.VMEM((C_HEADS, QB, 1), F32),
                        pltpu.VMEM((C_HEADS, QB, C_V), F32)],
        compiler_params=_cparams("parallel", "arbitrary"),
        name="mla_prompt",
    )(qc, kc.reshape(bsz, t, CW), vc.reshape(bsz, t, 256), cm)


def _pool_body(u_ref, prev_ref, cnt_ref, pw_ref, ps_ref, y_ref, ns_ref, buf, *, sb, t):
    u = u_ref[...]
    buf[:, 0:1, :] = jnp.zeros((sb, 1, D_POOL), F32)
    buf[:, 1:16, :] = prev_ref[...]
    buf[:, 16:16 + t, :] = u
    sums = {}
    run = u
    for k in range(1, 16):
        run = run + buf[:, 16 - k:16 - k + t, :]
        if k + 1 in POOL_WINDOWS:
            sums[k + 1] = run
    lane = lax.broadcasted_iota(jnp.int32, (sb, t, D_POOL), 2)
    tot = jnp.where(lane < 64, sums[2],
                    jnp.where(lane < 128, sums[4], jnp.where(lane < 192, sums[8], sums[16])))
    d = (tot / cnt_ref[...][None] - u).astype(BF16)
    y = _dot(d.reshape(sb * t, D_POOL), pw_ref[...]) * ps_ref[...]
    y_ref[...] = y.astype(BF16)
    ns_ref[...] = buf[:, t + 1:t + 16, :]


def _pool(zs, prev, cnt, pw, ps, sb, t):
    n = zs.shape[0]
    nseq = n // t
    return pl.pallas_call(
        functools.partial(_pool_body, sb=sb, t=t),
        grid=(nseq // sb,),
        in_specs=[pl.BlockSpec((sb, t, D_POOL), lambda i: (i, 0, O_DU // D_POOL)),
                  pl.BlockSpec((sb, POOL_STATE, D_POOL), lambda i: (i, 0, 0)),
                  _resident(cnt.shape), _resident(pw.shape), _resident(ps.shape)],
        out_specs=[pl.BlockSpec((sb * t, D_POOL), lambda i: (i, 0)),
                   pl.BlockSpec((sb, POOL_STATE, D_POOL), lambda i: (i, 0, 0))],
        out_shape=[jax.ShapeDtypeStruct((n, D_POOL), BF16),
                   jax.ShapeDtypeStruct((nseq, POOL_STATE, D_POOL), F32)],
        scratch_shapes=[pltpu.VMEM((sb, t + 16, D_POOL), F32)],
        compiler_params=_cparams("parallel"),
        name="pool",
    )(zs.reshape(nseq, t, ZS), prev, cnt, pw, ps)


def _merge_body(x_ref, a_ref, b_ref, c_ref, d_ref, gl_ref, wb_ref, wo_ref, o_ref):
    merged = None
    for n, r in enumerate((a_ref, b_ref, c_ref, d_ref)):
        gate = jax.nn.sigmoid(gl_ref[:, n * D_MODEL:(n + 1) * D_MODEL])
        term = gate * _dot(r[...], wb_ref[n])
        merged = term if merged is None else merged + term
    o_ref[...] = x_ref[...] + _dot(merged.astype(BF16), wo_ref[...])


def _merge(x, oa, ob, oc, od, gl, wb, wo, tm):
    n = x.shape[0]
    br = pl.BlockSpec((tm, 256), lambda i: (i, 0))
    return pl.pallas_call(
        _merge_body,
        grid=(n // tm,),
        in_specs=[pl.BlockSpec((tm, D_MODEL), lambda i: (i, 0)), br, br, br, br,
                  pl.BlockSpec((tm, 4 * D_MODEL), lambda i: (i, 0)),
                  _resident(wb.shape), _resident(wo.shape)],
        out_specs=pl.BlockSpec((tm, D_MODEL), lambda i: (i, 0)),
        out_shape=jax.ShapeDtypeStruct((n, D_MODEL), F32),
        compiler_params=_cparams("parallel"),
        name="merge",
    )(x, oa, ob, oc, od, gl, wb, wo)


FC = 256
NF = D_FF // FC


def _ffn_body(h_ref, st_ref, g_ref, wup_ref, wdn_ref, cw_ref, cb_ref, o_ref, ns_ref,
              buf, carry, *, sb, t, tiles_per_seq):
    i = pl.program_id(0)
    first = (i % tiles_per_seq) == 0
    h = h_ref[...]
    hn = (h * lax.rsqrt(jnp.mean(h * h, axis=-1, keepdims=True) + EPS) * g_ref[...]).astype(BF16)
    acc = jnp.zeros((sb * t, D_MODEL), F32)
    for f in range(NF):
        sl = slice(f * FC, (f + 1) * FC)
        ga = _dot(hn, wup_ref[:, sl]).reshape(sb, t, FC)
        va = _dot(hn, wup_ref[:, D_FF + f * FC:D_FF + (f + 1) * FC]).reshape(sb, t, FC)
        @pl.when(first)
        def _(sl=sl):
            buf[:, 6:8, :] = st_ref[:, :, sl]

        @pl.when(jnp.logical_not(first))
        def _(f=f):
            buf[:, 6:8, :] = carry[f]

        buf[:, 8:8 + t, :] = ga
        conv = (cb_ref[:, sl][None] + buf[:, 6:6 + t, :] * cw_ref[0:1, sl][None]
                + buf[:, 7:7 + t, :] * cw_ref[1:2, sl][None] + ga * cw_ref[2:3, sl][None])
        act = (conv * jax.nn.sigmoid(conv) * va).reshape(sb * t, FC).astype(BF16)
        acc = acc + _dot(act, wdn_ref[sl, :])
        last = buf[:, 6 + t:8 + t, :]
        carry[f] = last
        ns_ref[:, :, sl] = last
    o_ref[...] = h + acc


def _ffn(h, st, g, wup, wdn, cw, cb, sb, t, tiles_per_seq):
    n = h.shape[0]
    tm = sb * t
    nst = st.shape[0]
    return pl.pallas_call(
        functools.partial(_ffn_body, sb=sb, t=t, tiles_per_seq=tiles_per_seq),
        grid=(n // tm,),
        in_specs=[pl.BlockSpec((tm, D_MODEL), lambda i: (i, 0)),
                  pl.BlockSpec((sb, 2, D_FF), lambda i: (i // tiles_per_seq, 0, 0)),
                  _resident((1, D_MODEL)), _resident(wup.shape), _resident(wdn.shape),
                  _resident(cw.shape), _resident(cb.shape)],
        out_specs=[pl.BlockSpec((tm, D_MODEL), lambda i: (i, 0)),
                   pl.BlockSpec((sb, 2, D_FF), lambda i: (i // tiles_per_seq, 0, 0))],
        out_shape=[jax.ShapeDtypeStruct((n, D_MODEL), F32),
                   jax.ShapeDtypeStruct((nst, 2, D_FF), F32)],
        scratch_shapes=[pltpu.VMEM((sb, t + 8, FC), F32), pltpu.VMEM((NF, sb, 2, FC), F32)],
        compiler_params=_cparams("arbitrary"),
        name="ffn",
    )(h, st, g, wup, wdn, cw, cb)


def _page_specs(cache, layer, pp, width):
    return [pl.BlockSpec((None, None, PAGE, width),
                         lambda b, g, pt, k=k: (layer, pt[b, g * pp + k], 0, 0))
            for k in range(pp)]


def _sres(shape):
    nd = len(shape)
    return pl.BlockSpec(shape, lambda b, g, pt: (0,) * nd, pipeline_mode=pl.Buffered(1))


def _moba_sample_body(pt_ref, q_ref, new_ref, place_ref, als_ref, ald_ref, sc_ref, *rest,
                      pp, ts, past_len, nblk):
    pages = rest[:pp]
    o_ref = rest[pp]
    qbd, m_ref, l_ref, acc_ref, ks_ref = rest[pp + 1:]
    g = pl.program_id(1)
    rows = A_HEADS * ts
    ppb = MOBA_BLOCK // PAGE

    @pl.when(g == 0)
    def _():
        q = q_ref[...]
        for h in range(A_HEADS):
            qbd[h * ts:(h + 1) * ts, :] = _dot(q, place_ref[h]).astype(BF16)
        ks_ref[...] = jnp.zeros(ks_ref.shape, F32)

    qv = qbd[...]
    for nl in range(pp // ppb):
        blk = g * (pp // ppb) + nl
        r16 = [pages[nl * ppb + e][...].astype(BF16) for e in range(ppb)]
        ss = []
        for e in range(ppb):
            far = (past_len - (g * pp + nl * ppb + e) * PAGE).astype(F32)
            ss.append(_dot_nt(qv, r16[e]) + als_ref[...] - sc_ref[...] * far)
        s = jnp.concatenate(ss, axis=-1)
        m = jnp.max(s, axis=-1, keepdims=True)
        p = jnp.exp(s - m).astype(BF16)
        m_ref[blk] = m
        l_ref[blk] = jnp.sum(p.astype(F32), axis=-1, keepdims=True)
        acc = _dot(p[:, 0:PAGE], r16[0])
        for e in range(1, ppb):
            acc = acc + _dot(p[:, e * PAGE:(e + 1) * PAGE], r16[e])
        acc_ref[blk] = acc
        ksum = jnp.sum(pages[nl * ppb][...], axis=0, keepdims=True)
        for e in range(1, ppb):
            ksum = ksum + jnp.sum(pages[nl * ppb + e][...], axis=0, keepdims=True)
        ks_ref[pl.ds(blk, 1), :] = ksum

    @pl.when(g == pl.num_programs(1) - 1)
    def _():
        hi, lo = _split16(ks_ref[...] * (1.0 / MOBA_BLOCK))
        bs = _dot_nt(qv, hi) + _dot_nt(qv, lo)
        lane = lax.broadcasted_iota(jnp.int32, (rows, 128), 1)
        rank = jnp.zeros((rows, 128), F32)
        for mblk in range(nblk):
            col = bs[:, mblk:mblk + 1]
            beats = jnp.where(col > bs, 1.0, jnp.where((col == bs) & (lane > mblk), 1.0, 0.0))
            rank = rank + beats
        sel = jnp.where((lane < nblk) & (rank < MOBA_TOPK), 1.0, 0.0)
        new16 = jnp.concatenate([new_ref[...], jnp.zeros((PAGE - ts, 256), F32)], axis=0).astype(BF16)
        s_own = _dot_nt(qv, new16) + ald_ref[...]
        m_own = jnp.max(s_own, axis=-1, keepdims=True)
        p_own = jnp.exp(s_own - m_own).astype(BF16)
        mf = m_own
        for n in range(nblk):
            mf = jnp.maximum(mf, jnp.where(sel[:, n:n + 1] > 0.5, m_ref[n], NEG))
        lsum = jnp.exp(m_own - mf) * jnp.sum(p_own.astype(F32), axis=-1, keepdims=True)
        acc = jnp.exp(m_own - mf) * _dot(p_own, new16)
        for n in range(nblk):
            w = jnp.where(sel[:, n:n + 1] > 0.5, jnp.exp(m_ref[n] - mf), 0.0)
            lsum = lsum + w * l_ref[n]
            acc = acc + w * acc_ref[n]
        o = acc / lsum
        hpk = A_HEADS // A_KV
        for h in range(A_HEADS):
            j = h // hpk
            o_ref[:, h * 64:(h + 1) * 64] = o[h * ts:(h + 1) * ts, j * 128 + 64:(j + 1) * 128].astype(BF16)


def _moba_sample(qa, a_new, cache, layer, page_table, consts, bs, ts, pp):
    n_pages = page_table.shape[1]
    past_len = n_pages * PAGE
    nblk = past_len // MOBA_BLOCK
    rows = A_HEADS * ts
    place, als, ald, sc = consts["sa_place"], consts["sa_als"], consts["sa_ald"], consts["sa_sc"]
    grid_spec = pltpu.PrefetchScalarGridSpec(
        num_scalar_prefetch=1,
        grid=(bs, n_pages // pp),
        in_specs=[pl.BlockSpec((ts, 256), lambda b, g, pt: (b, 0)),
                  pl.BlockSpec((ts, 256), lambda b, g, pt: (b, 0)),
                  _sres(place.shape), _sres(als.shape), _sres(ald.shape), _sres(sc.shape)]
        + _page_specs(cache, layer, pp, 256),
        out_specs=pl.BlockSpec((ts, 256), lambda b, g, pt: (b, 0)),
        scratch_shapes=[pltpu.VMEM((rows, 256), BF16),
                        pltpu.VMEM((nblk, rows, 1), F32), pltpu.VMEM((nblk, rows, 1), F32),
                        pltpu.VMEM((nblk, rows, 256), F32), pltpu.VMEM((128, 256), F32)],
    )
    return pl.pallas_call(
        functools.partial(_moba_sample_body, pp=pp, ts=ts, past_len=past_len, nblk=nblk),
        grid_spec=grid_spec,
        out_shape=jax.ShapeDtypeStruct((bs * ts, 256), BF16),
        compiler_params=_cparams("parallel", "arbitrary"),
        name="moba_sample",
    )(page_table, qa, a_new, place, als, ald, sc, *([cache] * pp))


def _diff_sample_body(pt_ref, q_ref, new_ref, place_ref, lam_ref, gs_ref, als_ref, ald_ref, sc_ref,
                      *rest, pp, ts, past_len, lam_init):
    pages = rest[:pp]
    o_ref = rest[pp]
    qbd, m_ref, l_ref, acc_ref = rest[pp + 1:]
    g = pl.program_id(1)
    half = B_HEADS * ts

    @pl.when(g == 0)
    def _():
        q = q_ref[...]
        for e in range(2 * B_HEADS):
            qbd[e * ts:(e + 1) * ts, :] = _dot(q, place_ref[e]).astype(BF16)
        m_ref[0] = jnp.full(m_ref.shape[1:], NEG, F32)
        l_ref[0] = jnp.zeros(l_ref.shape[1:], F32)
        acc_ref[0] = jnp.zeros(acc_ref.shape[1:], F32)

    qv = qbd[...]
    r16 = [pages[k][...].astype(BF16) for k in range(pp)]
    ss = []
    for k in range(pp):
        far = (past_len - (g * pp + k) * PAGE).astype(F32)
        ss.append(_dot_nt(qv, r16[k]) + als_ref[...] - sc_ref[...] * far)
    s = jnp.concatenate(ss, axis=-1)
    m_prev = m_ref[0]
    m_new = jnp.maximum(m_prev, jnp.max(s, axis=-1, keepdims=True))
    alpha = jnp.exp(m_prev - m_new)
    p = jnp.exp(s - m_new).astype(BF16)
    pv = _dot(p[:, 0:PAGE], r16[0])
    for k in range(1, pp):
        pv = pv + _dot(p[:, k * PAGE:(k + 1) * PAGE], r16[k])
    m_ref[0] = m_new
    l_ref[0] = alpha * l_ref[0] + jnp.sum(p.astype(F32), axis=-1, keepdims=True)
    acc_ref[0] = alpha * acc_ref[0] + pv

    @pl.when(g == pl.num_programs(1) - 1)
    def _():
        new16 = jnp.concatenate([new_ref[...], jnp.zeros((PAGE - ts, 128), F32)], axis=0).astype(BF16)
        s_own = _dot_nt(qv, new16) + ald_ref[...]
        _flash_next(s_own, new16, m_ref, l_ref, acc_ref, 0)
        o = acc_ref[0] / l_ref[0]
        lam = _lam_of(lam_ref[...], lam_init)
        o = o[0:half, 64:128] - lam * o[half:2 * half, 64:128]
        o = o * lax.rsqrt(jnp.mean(o * o, axis=-1, keepdims=True) + EPS) * gs_ref[...] * (1.0 - lam_init)
        for h in range(B_HEADS):
            o_ref[:, h * 64:(h + 1) * 64] = o[h * ts:(h + 1) * ts].astype(BF16)


def _diff_sample(qb, b_new, b_lam, g_sub, cache, layer, page_table, consts, bs, ts, pp, lam_init):
    n_pages = page_table.shape[1]
    past_len = n_pages * PAGE
    rows = 2 * B_HEADS * ts
    place, als, ald, sc = consts["sb_place"], consts["sb_als"], consts["sb_ald"], consts["sb_sc"]
    grid_spec = pltpu.PrefetchScalarGridSpec(
        num_scalar_prefetch=1,
        grid=(bs, n_pages // pp),
        in_specs=[pl.BlockSpec((ts, 256), lambda b, g, pt: (b, 0)),
                  pl.BlockSpec((ts, 128), lambda b, g, pt: (b, 0)),
                  _sres(place.shape), _sres(b_lam.shape), _sres(g_sub.shape),
                  _sres(als.shape), _sres(ald.shape), _sres(sc.shape)]
        + _page_specs(cache, layer, pp, 128),
        out_specs=pl.BlockSpec((ts, 256), lambda b, g, pt: (b, 0)),
        scratch_shapes=[pltpu.VMEM((rows, 128), BF16),
                        pltpu.VMEM((1, rows, 1), F32), pltpu.VMEM((1, rows, 1), F32),
                        pltpu.VMEM((1, rows, 128), F32)],
    )
    return pl.pallas_call(
        functools.partial(_diff_sample_body, pp=pp, ts=ts, past_len=past_len, lam_init=lam_init),
        grid_spec=grid_spec,
        out_shape=jax.ShapeDtypeStruct((bs * ts, 256), BF16),
        compiler_params=_cparams("parallel", "arbitrary"),
        name="diff_sample",
    )(page_table, qb, b_new, place, b_lam, g_sub, als, ald, sc, *([cache] * pp))


def _mla_sample_body(pt_ref, q_ref, new_ref, mq_ref, wext_ref, wuv_ref, cm_ref, *rest, pp, ts):
    pages = rest[:pp]
    o_ref = rest[pp]
    lhs, pbuf, nbuf, m_ref, l_ref, acc_ref = rest[pp + 1:]
    b = pl.program_id(0)
    g = pl.program_id(1)
    rows = C_HEADS * ts
    next_rows = C_HEADS * C_NOPE + C_ROPE

    @pl.when((b == 0) & (g == 0))
    def _():
        pbuf[...] = jnp.zeros(pbuf.shape, BF16)
        nbuf[...] = jnp.zeros(nbuf.shape, BF16)
        lhs[rows:rows + next_rows, :] = wext_ref[...]

    @pl.when(g == 0)
    def _():
        q = q_ref[...]
        for h in range(C_HEADS):
            lhs[h * ts:(h + 1) * ts, :] = _dot(q, mq_ref[h]).astype(BF16)
        m_ref[0] = jnp.full(m_ref.shape[1:], NEG, F32)
        l_ref[0] = jnp.zeros(l_ref.shape[1:], F32)
        acc_ref[0] = jnp.zeros(acc_ref.shape[1:], F32)

    def scores(keys16):
        out = _dot_nt(lhs[...], keys16)
        kt = out[rows:rows + next_rows]
        kt2 = kt * kt
        ssq = jnp.sum(kt2[0:C_HEADS * C_NOPE].reshape(C_HEADS, C_NOPE, PAGE), axis=1)
        ssq = ssq + jnp.sum(kt2[C_HEADS * C_NOPE:], axis=0, keepdims=True)
        r = lax.rsqrt(ssq * (1.0 / C_QK) + EPS)
        r_rows = jnp.broadcast_to(r[:, None, :], (C_HEADS, ts, PAGE)).reshape(rows, PAGE)
        return out[0:rows] * r_rows

    ss = []
    for k in range(pp):
        pbuf[k, :, 0:160] = pages[k][...].astype(BF16)
        ss.append(scores(pbuf[k]))
    s = jnp.concatenate(ss, axis=-1)
    m_prev = m_ref[0]
    m_new = jnp.maximum(m_prev, jnp.max(s, axis=-1, keepdims=True))
    alpha = jnp.exp(m_prev - m_new)
    p = jnp.exp(s - m_new).astype(BF16)
    pv = _dot(p[:, 0:PAGE], pbuf[0])
    for k in range(1, pp):
        pv = pv + _dot(p[:, k * PAGE:(k + 1) * PAGE], pbuf[k])
    m_ref[0] = m_new
    l_ref[0] = alpha * l_ref[0] + jnp.sum(p.astype(F32), axis=-1, keepdims=True)
    acc_ref[0] = alpha * acc_ref[0] + pv

    @pl.when(g == pl.num_programs(1) - 1)
    def _():
        nbuf[0:ts, 0:160] = new_ref[...].astype(BF16)
        s_own = scores(nbuf[...]) + cm_ref[...]
        _flash_next(s_own, nbuf[...], m_ref, l_ref, acc_ref, 0)
        o_lat = (acc_ref[0][:, 0:C_KVL] / l_ref[0]).astype(BF16)
        for h in range(C_HEADS):
            o_ref[:, h * C_V:(h + 1) * C_V] = _dot(
                o_lat[h * ts:(h + 1) * ts], wuv_ref[:, h * C_V:(h + 1) * C_V]).astype(BF16)


def _mla_sample(qce, c_new, mq, wext, wuv, cache, layer, page_table, consts, bs, ts, pp):
    n_pages = page_table.shape[1]
    rows = C_HEADS * ts
    cm = consts["sc_cm"]
    grid_spec = pltpu.PrefetchScalarGridSpec(
        num_scalar_prefetch=1,
        grid=(bs, n_pages // pp),
        in_specs=[pl.BlockSpec((ts, CW), lambda b, g, pt: (b, 0)),
                  pl.BlockSpec((ts, 160), lambda b, g, pt: (b, 0)),
                  _sres(mq.shape), _sres(wext.shape), _sres(wuv.shape), _sres(cm.shape)]
        + _page_specs(cache, layer, pp, 160),
        out_specs=pl.BlockSpec((ts, 256), lambda b, g, pt: (b, 0)),
        scratch_shapes=[pltpu.VMEM((rows + wext.shape[0], 256), BF16),
                        pltpu.VMEM((pp, PAGE, 256), BF16), pltpu.VMEM((PAGE, 256), BF16),
                        pltpu.VMEM((1, rows, 1), F32), pltpu.VMEM((1, rows, 1), F32),
                        pltpu.VMEM((1, rows, 256), F32)],
    )
    return pl.pallas_call(
        functools.partial(_mla_sample_body, pp=pp, ts=ts),
        grid_spec=grid_spec,
        out_shape=jax.ShapeDtypeStruct((bs * ts, 256), BF16),
        compiler_params=_cparams("arbitrary", "arbitrary"),
        name="mla_sample",
    )(page_table, qce, c_new, mq, wext, wuv, cm, *([cache] * pp))


def _alibi(n):
    return np.asarray(2.0 ** (-8.0 * np.arange(1, n + 1) / n), dtype=np.float32)


def _group_matrix(width, group):
    idx = np.arange(width) // group
    return jnp.asarray(idx[:, None] == idx[None, :], dtype=BF16)


def _rope_tables(pos, reps):
    half = C_ROPE // 2
    inv = ROPE_THETA ** (-jnp.arange(half, dtype=F32) / half)
    ang = pos.astype(F32)[:, None] * inv[None, :]
    cos, sin = jnp.cos(ang), jnp.sin(ang)
    n = pos.shape[0]
    one, zero = jnp.ones((n, C_NOPE), F32), jnp.zeros((n, C_NOPE), F32)
    zh = jnp.zeros((n, half), F32)
    zp = jnp.zeros((n, CP - C_QK), F32)
    qcos = jnp.concatenate([one, cos, cos, zp] * C_HEADS, axis=-1)
    qsa = jnp.concatenate([zero, -sin, zh, zp] * C_HEADS, axis=-1)
    qsb = jnp.concatenate([zero, zh, sin, zp] * C_HEADS, axis=-1)
    pad = jnp.zeros((n, 128 - C_ROPE), F32)
    kcos = jnp.concatenate([cos, cos, pad], axis=-1)
    ksa = jnp.concatenate([-sin, zh, pad], axis=-1)
    ksb = jnp.concatenate([zh, sin, pad], axis=-1)
    rt = jnp.concatenate([qcos, qsa, qsb, kcos, ksa, ksb], axis=-1)
    return jnp.tile(rt, (reps, 1))


def _pool_counts(pos):
    win = np.repeat(np.asarray(POOL_WINDOWS), POOL_GROUP)[None, :]
    return jnp.asarray(np.minimum(win, np.asarray(pos)[:, None] + 1), dtype=F32)


def _make_consts(tp, ts, past_len, tm_s):
    c = {}
    sl = _alibi(A_HEADS)
    i = np.arange(QB)[:, None]
    r = np.arange(QB)[None, :]
    dist = (i - r).astype(np.float32)
    causal = i >= r

    def alibi_tables(heads):
        al0 = np.concatenate([-sl[h] * dist for h in heads], axis=0)
        ald = np.concatenate([np.where(causal, -sl[h] * dist, NEG) for h in heads], axis=0)
        sc = np.concatenate([np.full((QB, 1), sl[h], np.float32) for h in heads], axis=0)
        return ald.astype(np.float32), al0.astype(np.float32), sc

    hpk = A_HEADS // A_KV
    per = [alibi_tables(range(j * hpk, (j + 1) * hpk)) for j in range(A_KV)]
    c["a_ald"], c["a_al0"], c["a_sc"] = (jnp.asarray(np.stack([p[k] for p in per])) for k in range(3))
    c["b_ald"], c["b_al0"], c["b_sc"] = (jnp.asarray(x) for x in alibi_tables(range(B_HEADS)))
    c["c_cm"] = jnp.asarray(np.where(causal, 0.0, NEG).astype(np.float32))
    nb = tp // MOBA_BLOCK
    assert nb + MOBA_TOPK <= 32 and tp % QB == 0
    t_idx = np.arange(tp)[:, None] // MOBA_BLOCK
    lane = np.arange(128)[None, :]
    c["a_avg"] = jnp.asarray(np.where(t_idx == lane % 32, 1.0 / MOBA_BLOCK, 0.0), dtype=BF16)
    c["rt_p"] = _rope_tables(jnp.arange(tp, dtype=jnp.int32), 1)
    c["rt_s"] = _rope_tables(past_len + jnp.arange(ts, dtype=jnp.int32), tm_s // ts)
    c["cnt_p"] = _pool_counts(np.arange(tp))
    c["cnt_s"] = _pool_counts(past_len + np.arange(ts))
    c["g64"], c["g32"], c["g96"] = _group_matrix(256, 64), _group_matrix(256, 32), _group_matrix(CW, CP)

    assert past_len % MOBA_BLOCK == 0 and ts <= PAGE and MOBA_BLOCK % PAGE == 0
    qi = np.tile(np.arange(ts), A_HEADS)[:, None]
    kr = np.arange(PAGE)[None, :]
    slr = np.repeat(sl, ts)[:, None]
    als = (-slr * (qi - kr)).astype(np.float32)
    own_ok = (kr <= qi) & (kr < ts)
    ald = np.where(own_ok, als, NEG).astype(np.float32)
    c["sa_als"], c["sa_ald"], c["sa_sc"] = jnp.asarray(als), jnp.asarray(ald), jnp.asarray(slr)
    c["sb_als"], c["sb_ald"], c["sb_sc"] = (jnp.asarray(np.concatenate([x, x], axis=0))
                                            for x in (als, ald, slr))
    c["sc_cm"] = jnp.asarray(np.where(own_ok, 0.0, NEG).astype(np.float32))
    pa = np.zeros((A_HEADS, 256, 256), np.float32)
    for h in range(A_HEADS):
        j = h // hpk
        pa[h, h * 64 + np.arange(64), j * 128 + np.arange(64)] = 1.0
    c["sa_place"] = jnp.asarray(pa, dtype=BF16)
    pb = np.zeros((2 * B_HEADS, 256, 128), np.float32)
    for e in range(2):
        for h in range(B_HEADS):
            pb[e * B_HEADS + h, h * 64 + e * 32 + np.arange(32), e * 32 + np.arange(32)] = 1.0
    c["sb_place"] = jnp.asarray(pb, dtype=BF16)
    return c


def _layer_weights(l, p):
    w = {}
    w_in = p["w_in"][l]
    sizes = (256, 128, 128, 256, 64, 64, 256, 160, 256, 4096)
    offs = np.concatenate([[0], np.cumsum(sizes)])
    aq, ak, av, bq, bk, bv, cq, ckv, du, gl = (w_in[:, offs[k]:offs[k + 1]] for k in range(10))
    akv = jnp.concatenate([ak[:, 0:64], av[:, 0:64], ak[:, 64:128], av[:, 64:128]], axis=1)
    pad = jnp.zeros((D_MODEL, 96), F32)
    w["ws"] = jnp.concatenate([aq, akv, bq, cq, du, ckv, pad, bk, bv], axis=1).astype(BF16)
    w["wg"] = gl.astype(BF16)
    w["g1"] = p["norm1_g"][l][None]
    w["g2"] = p["norm2_g"][l][None]

    def row(v):
        return jnp.pad(v, (0, CW - v.shape[0]))[None]

    def heads_padded(v):
        return jnp.tile(jnp.pad(v, (0, CP - C_QK)), C_HEADS)

    one64 = jnp.ones((64,), F32)
    zero64 = jnp.zeros((64,), F32)
    gains = [
        row(jnp.tile(p["a_q_g"][l], 4) * (A_DIM ** -0.5)),
        row(jnp.concatenate([p["a_k_g"][l], one64] * 2)),
        row(jnp.concatenate([one64, zero64] * 2)),
        row(jnp.tile(p["b_q_g"][l], 8) * (B_QK ** -0.5)),
        row(jnp.concatenate([p["b_k_g"][l], p["b_k_g"][l], one64])),
        row(jnp.concatenate([one64, zero64])),
        row(p["c_qa_g"][l]),
        row(heads_padded(p["c_q_g"][l]) * (C_QK ** -0.5)),
        row(p["c_kva_g"][l]),
        row(heads_padded(p["c_k_g"][l])),
    ]
    w["gains"] = jnp.concatenate(gains + [jnp.zeros((16 - len(gains), CW), F32)], axis=0)
    w["wuq"] = jnp.pad(p["w_uq"][l], ((0, 0), (0, 0), (0, CP - C_QK))).reshape(C_QL, CW).astype(BF16)
    w_uk = p["w_uk"][l]
    eye = jnp.eye(C_ROPE, dtype=F32)
    top = jnp.concatenate([jnp.concatenate([w_uk[:, h], jnp.zeros((C_KVL, CP - C_NOPE), F32)], axis=1)
                           for h in range(C_HEADS)], axis=1)
    mid = jnp.concatenate([jnp.concatenate([jnp.zeros((C_ROPE, C_NOPE), F32), eye,
                                            jnp.zeros((C_ROPE, CP - C_QK), F32)], axis=1)
                           for _ in range(C_HEADS)], axis=1)
    w["wkc"] = jnp.concatenate([top, mid, jnp.zeros((96, CW), F32)], axis=0).astype(BF16)
    w["wuv"] = p["w_uv"][l].reshape(C_KVL, C_HEADS * C_V).astype(BF16)
    mq = []
    for h in range(C_HEADS):
        m = jnp.zeros((CW, 256), F32)
        m = m.at[h * CP:h * CP + C_NOPE, 0:C_KVL].set(w_uk[:, h].T)
        m = m.at[h * CP + C_NOPE:h * CP + C_QK, C_KVL:C_KVL + C_ROPE].set(eye)
        mq.append(m)
    w["mq"] = jnp.stack(mq).astype(BF16)
    probe = jnp.concatenate([w_uk[:, h].T for h in range(C_HEADS)], axis=0)
    probe = jnp.concatenate([probe, jnp.zeros((C_HEADS * C_NOPE, 128), F32)], axis=1)
    rope_rows = jnp.zeros((C_ROPE, 256), F32).at[:, C_KVL:C_KVL + C_ROPE].set(eye)
    w["wext"] = jnp.concatenate([probe, rope_rows], axis=0).astype(BF16)
    pw = jnp.zeros((D_POOL, D_POOL), F32)
    for gi in range(len(POOL_WINDOWS)):
        sl = slice(gi * POOL_GROUP, (gi + 1) * POOL_GROUP)
        pw = pw.at[sl, sl].set(p["pool_w"][l][gi])
    w["pw"] = pw.astype(BF16)
    w["ps"] = p["pool_scale"][l][None]
    w["wb"] = p["w_branch"][l].astype(BF16)
    w["wo"] = p["w_out"][l].astype(BF16)
    w["wup"] = p["w_up"][l].astype(BF16)
    w["wdn"] = p["w_down"][l].astype(BF16)
    w["cw"] = p["conv_w"][l]
    w["cb"] = p["conv_b"][l][None]
    w["b_lam"] = p["b_lam"][l]
    w["g_sub"] = p["b_sub_g"][l][None]
    return w


def _tile(n, pref):
    return pref if n % pref == 0 else n


def kernel(x_prompt, x_sample, cache_a, cache_b, cache_c, state_pool, state_conv, page_table,
           norm1_g, w_in, a_q_g, a_k_g, b_q_g, b_k_g, b_lam, b_sub_g, c_qa_g, c_kva_g,
           w_uq, w_uk, w_uv, c_q_g, c_k_g, pool_w, pool_scale, w_branch, w_out,
           norm2_g, w_up, conv_w, conv_b, w_down):
    params = dict(norm1_g=norm1_g, w_in=w_in, a_q_g=a_q_g, a_k_g=a_k_g, b_q_g=b_q_g, b_k_g=b_k_g,
                  b_lam=b_lam, b_sub_g=b_sub_g, c_qa_g=c_qa_g, c_kva_g=c_kva_g, w_uq=w_uq,
                  w_uk=w_uk, w_uv=w_uv, c_q_g=c_q_g, c_k_g=c_k_g, pool_w=pool_w,
                  pool_scale=pool_scale, w_branch=w_branch, w_out=w_out, norm2_g=norm2_g,
                  w_up=w_up, conv_w=conv_w, conv_b=conv_b, w_down=w_down)
    bp, tp, _ = x_prompt.shape
    bs, ts, _ = x_sample.shape
    depth = w_in.shape[0]
    n_pages = page_table.shape[1]
    past_len = n_pages * PAGE
    n_s = bs * ts
    tm_p = _tile(tp, 256)
    tm_s = _tile(n_s, 256)
    tf_p = _tile(tp, 512)
    sb_s = tm_s // ts
    pp = 16 if n_pages % 16 == 0 else n_pages
    consts = _make_consts(tp, ts, past_len, tm_s)
    n_pool = cache_a.shape[1]
    ca = cache_a.reshape(depth, n_pool, PAGE, 256)
    cb = cache_b.reshape(depth, n_pool, PAGE, 128)
    zero_pool = jnp.zeros((bp, POOL_STATE, D_POOL), F32)
    zero_conv = jnp.zeros((bp, 2, D_FF), F32)

    xp = x_prompt.reshape(bp * tp, D_MODEL)
    xs = x_sample.reshape(n_s, D_MODEL)
    new_p = [[] for _ in range(5)]
    new_s = [[] for _ in range(5)]
    for l in range(depth):
        w = _layer_weights(l, params)
        lam_init = 0.8 - 0.6 * math.exp(-0.3 * l)
        zs, gl = _inproj(xp, w["g1"], w["ws"], w["wg"], tm_p)
        qa, a_new, qb, b_new, qc, _, c_new, kc, vc = _prep(
            zs, consts["rt_p"], w["gains"], consts["g64"], consts["g32"], consts["g96"],
            w["wuq"], w["wkc"], w["wuv"], tm_p)
        oa = _moba_prompt(qa, a_new, consts, bp, tp)
        ob = _diff_prompt(qb, b_new, w["b_lam"], w["g_sub"], consts, bp, tp, lam_init)
        oc = _mla_prompt(qc, kc, vc, consts, bp, tp)
        od, pool_new = _pool(zs, zero_pool, consts["cnt_p"], w["pw"], w["ps"], 1, tp)
        h = _merge(xp, oa, ob, oc, od, gl, w["wb"], w["wo"], tm_p)
        xp, conv_new = _ffn(h, zero_conv, w["g2"], w["wup"], w["wdn"], w["cw"], w["cb"],
                            1, tf_p, tp // tf_p)
        for k, v in enumerate((a_new.reshape(bp, tp, A_KV, 128), b_new.reshape(bp, tp, 1, 128),
                               c_new.reshape(bp, tp, 160), pool_new, conv_new)):
            new_p[k].append(v)
        zs, gl = _inproj(xs, w["g1"], w["ws"], w["wg"], tm_s)
        qa, a_new, qb, b_new, _, qce, c_new, _, _ = _prep(
            zs, consts["rt_s"], w["gains"], consts["g64"], consts["g32"], consts["g96"],
            w["wuq"], w["wkc"], w["wuv"], tm_s)
        oa = _moba_sample(qa, a_new, ca, l, page_table, consts, bs, ts, pp)
        ob = _diff_sample(qb, b_new, w["b_lam"], w["g_sub"], cb, l, page_table, consts, bs, ts, pp,
                          lam_init)
        oc = _mla_sample(qce, c_new, w["mq"], w["wext"], w["wuv"], cache_c, l, page_table, consts,
                         bs, ts, pp)
        od, pool_new = _pool(zs, state_pool[l], consts["cnt_s"], w["pw"], w["ps"], sb_s, ts)
        h = _merge(xs, oa, ob, oc, od, gl, w["wb"], w["wo"], tm_s)
        xs, conv_new = _ffn(h, state_conv[l], w["g2"], w["wup"], w["wdn"], w["cw"], w["cb"],
                            sb_s, ts, 1)
        for k, v in enumerate((a_new.reshape(bs, ts, A_KV, 128), b_new.reshape(bs, ts, 1, 128),
                               c_new.reshape(bs, ts, 160), pool_new, conv_new)):
            new_s[k].append(v)
    np_ = [jnp.stack(v, axis=0) for v in new_p]
    ns_ = [jnp.stack(v, axis=0) for v in new_s]
    return (xp.reshape(bp, tp, D_MODEL), xs.reshape(bs, ts, D_MODEL),
            np_[0], np_[1], np_[2], np_[3], np_[4], ns_[0], ns_[1], ns_[2], ns_[3], ns_[4])
```

```python
import functools
import math

import numpy as np
import jax
import jax.numpy as jnp
from jax import lax
from jax.experimental import pallas as pl
from jax.experimental.pallas import tpu as pltpu

F32 = jnp.float32
BF16 = jnp.bfloat16

D_MODEL = 1024
PAGE = 128
A_HEADS, A_KV, A_DIM = 4, 2, 64
MOBA_BLOCK, MOBA_TOPK = 256, 3
B_HEADS, B_QK, B_V = 4, 32, 64
C_HEADS, C_QL, C_KVL, C_NOPE, C_ROPE, C_V = 4, 256, 128, 64, 32, 64
C_QK = C_NOPE + C_ROPE
CP = 128
CW = C_HEADS * CP
ROPE_THETA = 10000.0
POOL_WINDOWS = (2, 4, 8, 16)
POOL_GROUP = 64
D_POOL = 256
POOL_STATE = 15
D_FF = 2816
EPS = 1e-6
NEG = -1e30

ZS = 1664
O_AQ, O_AKV, O_BQ, O_CQ, O_DU, O_CKV, O_BKV = 0, 256, 512, 768, 1024, 1280, 1536

QB = 256
VMEM_LIMIT = 56 * 1024 * 1024


def _cparams(*sem):
    return pltpu.CompilerParams(dimension_semantics=sem, vmem_limit_bytes=VMEM_LIMIT)


def _resident(shape):
    nd = len(shape)
    return pl.BlockSpec(shape, lambda *_: (0,) * nd, pipeline_mode=pl.Buffered(1))


def _dot(a, b):
    return jnp.dot(a, b, preferred_element_type=F32)


def _dot_nt(a, b):
    return lax.dot_general(a, b, (((1,), (1,)), ((), ())), preferred_element_type=F32)


def _split16(x):
    hi = x.astype(BF16)
    lo = (x - hi.astype(F32)).astype(BF16)
    return hi, lo


def _inproj_body(x_ref, g_ref, ws_ref, wg_ref, zs_ref, gl_ref):
    x = x_ref[...]
    ms = jnp.mean(x * x, axis=-1, keepdims=True)
    xn = (x * lax.rsqrt(ms + EPS) * g_ref[...]).astype(BF16)
    zs_ref[...] = _dot(xn, ws_ref[...])
    for c in range(4):
        sl = slice(c * D_MODEL, (c + 1) * D_MODEL)
        gl_ref[:, sl] = _dot(xn, wg_ref[:, sl])


def _inproj(x, g, ws, wg, tm):
    n = x.shape[0]
    return pl.pallas_call(
        _inproj_body,
        grid=(n // tm,),
        in_specs=[pl.BlockSpec((tm, D_MODEL), lambda i: (i, 0)),
                  _resident((1, D_MODEL)), _resident(ws.shape), _resident(wg.shape)],
        out_specs=[pl.BlockSpec((tm, ZS), lambda i: (i, 0)),
                   pl.BlockSpec((tm, 4 * D_MODEL), lambda i: (i, 0))],
        out_shape=[jax.ShapeDtypeStruct((n, ZS), F32),
                   jax.ShapeDtypeStruct((n, 4 * D_MODEL), F32)],
        compiler_params=_cparams("parallel"),
        name="inproj",
    )(x, g, ws, wg)


G_AQ, G_AK, M_AK, G_BQ, G_BK, M_BK, G_CQA, G_CQ, G_CKVA, G_CK = range(10)


def _gnorm(x, gmat, n):
    ss = _dot((x * x).astype(BF16), gmat)
    return x * lax.rsqrt(ss * (1.0 / n) + EPS)


def _rope_cols(x, cos, sina, sinb):
    outs = []
    for c in range(x.shape[1] // 128):
        sl = slice(c * 128, (c + 1) * 128)
        xc = x[:, sl]
        outs.append(xc * cos[:, sl] + pltpu.roll(xc, 112, 1) * sina[:, sl]
                    + pltpu.roll(xc, 16, 1) * sinb[:, sl])
    return outs[0] if len(outs) == 1 else jnp.concatenate(outs, axis=-1)


def _prep_body(zs_ref, rt_ref, gn_ref, g64_ref, g32_ref, g96_ref, wuq_ref, wkc_ref, wuv_ref,
               qa_ref, an_ref, qb_ref, bn_ref, qc_ref, qce_ref, cn_ref, kc_ref, vc_ref):
    def gain(row, w):
        return gn_ref[row:row + 1, :w]

    aq = zs_ref[:, O_AQ:O_AQ + 256]
    qa_ref[...] = (_gnorm(aq, g64_ref[...], A_DIM) * gain(G_AQ, 256)).astype(BF16)
    akv = zs_ref[:, O_AKV:O_AKV + 256]
    an_ref[...] = jnp.where(gain(M_AK, 256) > 0.5,
                            _gnorm(akv, g64_ref[...], A_DIM) * gain(G_AK, 256), akv)
    bq = zs_ref[:, O_BQ:O_BQ + 256]
    qb_ref[...] = (_gnorm(bq, g32_ref[...], B_QK) * gain(G_BQ, 256)).astype(BF16)
    bkv = zs_ref[:, O_BKV:O_BKV + 128]
    bn_ref[...] = jnp.where(gain(M_BK, 128) > 0.5,
                            _gnorm(bkv, g32_ref[:128, :128], B_QK) * gain(G_BK, 128), bkv)
    cq = zs_ref[:, O_CQ:O_CQ + 256]
    cqn = cq * lax.rsqrt(jnp.mean(cq * cq, axis=-1, keepdims=True) + EPS) * gain(G_CQA, 256)
    qc = _dot(cqn.astype(BF16), wuq_ref[...])
    qc = _rope_cols(qc, rt_ref[:, 0:CW], rt_ref[:, CW:2 * CW], rt_ref[:, 2 * CW:3 * CW])
    qcn = _gnorm(qc, g96_ref[...], C_QK) * gain(G_CQ, CW)
    qc_ref[...] = qcn.astype(BF16)
    qce_ref[...] = (qcn * gain(G_CK, CW)).astype(BF16)
    lat = zs_ref[:, O_CKV:O_CKV + 128]
    latn = lat * lax.rsqrt(jnp.mean(lat * lat, axis=-1, keepdims=True) + EPS) * gain(G_CKVA, 128)
    kr = _rope_cols(zs_ref[:, O_CKV + 128:O_CKV + 256], rt_ref[:, 3 * CW:3 * CW + 128],
                    rt_ref[:, 3 * CW + 128:3 * CW + 256], rt_ref[:, 3 * CW + 256:3 * CW + 384])
    cn_ref[:, 0:128] = latn
    cn_ref[:, 128:160] = kr[:, 0:32]
    cfull = jnp.concatenate([latn, kr], axis=-1).astype(BF16)
    kc = _gnorm(_dot(cfull, wkc_ref[...]), g96_ref[...], C_QK) * gain(G_CK, CW)
    kc_ref[...] = kc.astype(BF16)
    vc_ref[...] = _dot(latn.astype(BF16), wuv_ref[...]).astype(BF16)


def _prep(zs, rt, gains, g64, g32, g96, wuq, wkc, wuv, tm):
    n = zs.shape[0]
    nt = rt.shape[0] // tm
    outs = [(256, BF16), (256, F32), (256, BF16), (128, F32), (CW, BF16), (CW, BF16),
            (160, F32), (CW, BF16), (256, BF16)]
    return pl.pallas_call(
        _prep_body,
        grid=(n // tm,),
        in_specs=[pl.BlockSpec((tm, ZS), lambda i: (i, 0)),
                  pl.BlockSpec((tm, rt.shape[1]), lambda i: (i % nt, 0)),
                  _resident(gains.shape), _resident(g64.shape), _resident(g32.shape),
                  _resident(g96.shape), _resident(wuq.shape), _resident(wkc.shape),
                  _resident(wuv.shape)],
        out_specs=[pl.BlockSpec((tm, w), lambda i: (i, 0)) for w, _ in outs],
        out_shape=[jax.ShapeDtypeStruct((n, w), dt) for w, dt in outs],
        compiler_params=_cparams("parallel"),
        name="prep",
    )(zs, rt, gains, g64, g32, g96, wuq, wkc, wuv)


def _flash_first(s, v, m_ref, l_ref, acc_ref, idx):
    m = jnp.max(s, axis=-1, keepdims=True)
    p = jnp.exp(s - m)
    m_ref[idx] = m
    l_ref[idx] = jnp.sum(p, axis=-1, keepdims=True)
    acc_ref[idx] = _dot(p.astype(BF16), v)


def _flash_next(s, v, m_ref, l_ref, acc_ref, idx):
    m_prev = m_ref[idx]
    m_new = jnp.maximum(m_prev, jnp.max(s, axis=-1, keepdims=True))
    alpha = jnp.exp(m_prev - m_new)
    p = jnp.exp(s - m_new)
    m_ref[idx] = m_new
    l_ref[idx] = alpha * l_ref[idx] + jnp.sum(p, axis=-1, keepdims=True)
    acc_ref[idx] = alpha * acc_ref[idx] + _dot(p.astype(BF16), v)


def _moba_prompt_body(q_ref, kv_ref, ald_ref, al0_ref, sc_ref, avg_ref, o_ref,
                      k16, v16, km_ref, sel_ref, m_ref, l_ref, acc_ref, *, nb):
    c = pl.program_id(1)

    @pl.when(c == 0)
    def _():
        kv = kv_ref[0]
        for j in range(A_KV):
            k16[j] = kv[:, j * 128:j * 128 + 64].astype(BF16)
            v16[j] = kv[:, j * 128 + 64:(j + 1) * 128].astype(BF16)
        hi, lo = _split16(kv.T)
        kmr = _dot(hi, avg_ref[...]) + _dot(lo, avg_ref[...])
        lane = lax.broadcasted_iota(jnp.int32, (64, 128), 1)
        rows = []
        for h in range(A_HEADS):
            j = h // (A_HEADS // A_KV)
            rows.append(jnp.where(lane // 32 == h, kmr[j * 128:j * 128 + 64, :], 0.0))
        km = jnp.concatenate(rows, axis=0)
        hi, lo = _split16(km)
        km_ref[0] = hi
        km_ref[1] = lo

    q = q_ref[...]
    bs = _dot(q, km_ref[0]) + _dot(q, km_ref[1])
    lane = lax.broadcasted_iota(jnp.int32, (QB, 128), 1)
    valid = (lane % 32) < c
    bsm = jnp.where(valid, bs, NEG)
    rank = jnp.zeros((QB, 128), F32)
    for k in range(1, nb):
        rank = rank + jnp.where(pltpu.roll(bsm, k, 1) >= bsm, 1.0, 0.0)
        rank = rank + jnp.where(pltpu.roll(bsm, 128 - k, 1) > bsm, 1.0, 0.0)
    sel_ref[...] = jnp.where(valid, jnp.where(rank < MOBA_TOPK, 1.0, 0.0), 0.0)

    start = pl.multiple_of(c * QB, QB)
    hpk = A_HEADS // A_KV
    for j in range(A_KV):
        qs = jnp.concatenate([q[:, (hpk * j + e) * 64:(hpk * j + e + 1) * 64] for e in range(hpk)],
                             axis=0)
        s = _dot_nt(qs, k16[j, pl.ds(start, QB), :]) + ald_ref[j]
        _flash_first(s, v16[j, pl.ds(start, QB), :], m_ref, l_ref, acc_ref, j)

        def body(n, carry, j=j, qs=qs):
            off = pl.multiple_of(n * QB, QB)
            far = ((c - n) * QB).astype(F32)
            s = _dot_nt(qs, k16[j, pl.ds(off, QB), :]) + al0_ref[j] - sc_ref[j] * far
            sel = sel_ref[...]
            cols = [jnp.max(jnp.where(lane == (hpk * j + e) * 32 + n, sel, 0.0),
                            axis=-1, keepdims=True) for e in range(hpk)]
            selcol = jnp.concatenate(cols, axis=0)
            s = jnp.where(selcol > 0.5, s, NEG)
            _flash_next(s, v16[j, pl.ds(off, QB), :], m_ref, l_ref, acc_ref, j)
            return carry

        lax.fori_loop(0, c, body, 0)
        o = acc_ref[j] / l_ref[j]
        for e in range(hpk):
            h = hpk * j + e
            o_ref[:, h * 64:(h + 1) * 64] = o[e * QB:(e + 1) * QB].astype(BF16)


def _moba_prompt(qa, a_new, consts, bsz, t):
    nq = t // QB
    nb = t // MOBA_BLOCK
    ald, al0, sc, avg = consts["a_ald"], consts["a_al0"], consts["a_sc"], consts["a_avg"]
    return pl.pallas_call(
        functools.partial(_moba_prompt_body, nb=nb),
        grid=(bsz, nq),
        in_specs=[pl.BlockSpec((QB, 256), lambda b, c: (b * nq + c, 0)),
                  pl.BlockSpec((1, t, 256), lambda b, c: (b, 0, 0)),
                  _resident(ald.shape), _resident(al0.shape), _resident(sc.shape),
                  _resident(avg.shape)],
        out_specs=pl.BlockSpec((QB, 256), lambda b, c: (b * nq + c, 0)),
        out_shape=jax.ShapeDtypeStruct((bsz * t, 256), BF16),
        scratch_shapes=[pltpu.VMEM((A_KV, t, 64), BF16), pltpu.VMEM((A_KV, t, 64), BF16),
                        pltpu.VMEM((2, 256, 128), BF16), pltpu.VMEM((QB, 128), F32),
                        pltpu.VMEM((A_KV, 2 * QB, 1), F32), pltpu.VMEM((A_KV, 2 * QB, 1), F32),
                        pltpu.VMEM((A_KV, 2 * QB, 64), F32)],
        compiler_params=_cparams("parallel", "arbitrary"),
        name="moba_prompt",
    )(qa, a_new.reshape(bsz, t, 256), ald, al0, sc, avg)


def _lam_of(lp, lam_init):
    a = jnp.sum(lp[0:1] * lp[1:2], axis=-1, keepdims=True)
    b = jnp.sum(lp[2:3] * lp[3:4], axis=-1, keepdims=True)
    return jnp.exp(a) - jnp.exp(b) + lam_init


def _diff_prompt_body(q_ref, kv_ref, lam_ref, gs_ref, ald_ref, al0_ref, sc_ref, o_ref,
                      k16, v16, m_ref, l_ref, acc_ref, *, lam_init):
    c = pl.program_id(1)

    @pl.when(c == 0)
    def _():
        kv = kv_ref[0]
        for i in range(2):
            k16[i] = kv[:, i * 32:(i + 1) * 32].astype(BF16)
        v16[...] = kv[:, 64:128].astype(BF16)

    q = q_ref[...]
    start = pl.multiple_of(c * QB, QB)
    for i in range(2):
        qs = jnp.concatenate([q[:, h * 64 + i * 32:h * 64 + (i + 1) * 32] for h in range(B_HEADS)],
                             axis=0)
        s = _dot_nt(qs, k16[i, pl.ds(start, QB), :]) + ald_ref[...]
        _flash_first(s, v16[pl.ds(start, QB), :], m_ref, l_ref, acc_ref, i)

        def body(n, carry, i=i, qs=qs):
            off = pl.multiple_of(n * QB, QB)
            far = ((c - n) * QB).astype(F32)
            s = _dot_nt(qs, k16[i, pl.ds(off, QB), :]) + al0_ref[...] - sc_ref[...] * far
            _flash_next(s, v16[pl.ds(off, QB), :], m_ref, l_ref, acc_ref, i)
            return carry

        lax.fori_loop(0, c, body, 0)

    lam = _lam_of(lam_ref[...], lam_init)
    o = acc_ref[0] / l_ref[0] - lam * (acc_ref[1] / l_ref[1])
    o = o * lax.rsqrt(jnp.mean(o * o, axis=-1, keepdims=True) + EPS) * gs_ref[...] * (1.0 - lam_init)
    for h in range(B_HEADS):
        o_ref[:, h * 64:(h + 1) * 64] = o[h * QB:(h + 1) * QB].astype(BF16)


def _diff_prompt(qb, b_new, b_lam, g_sub, consts, bsz, t, lam_init):
    nq = t // QB
    ald, al0, sc = consts["b_ald"], consts["b_al0"], consts["b_sc"]
    r = B_HEADS * QB
    return pl.pallas_call(
        functools.partial(_diff_prompt_body, lam_init=lam_init),
        grid=(bsz, nq),
        in_specs=[pl.BlockSpec((QB, 256), lambda b, c: (b * nq + c, 0)),
                  pl.BlockSpec((1, t, 128), lambda b, c: (b, 0, 0)),
                  _resident(b_lam.shape), _resident(g_sub.shape),
                  _resident(ald.shape), _resident(al0.shape), _resident(sc.shape)],
        out_specs=pl.BlockSpec((QB, 256), lambda b, c: (b * nq + c, 0)),
        out_shape=jax.ShapeDtypeStruct((bsz * t, 256), BF16),
        scratch_shapes=[pltpu.VMEM((2, t, 32), BF16), pltpu.VMEM((t, 64), BF16),
                        pltpu.VMEM((2, r, 1), F32), pltpu.VMEM((2, r, 1), F32),
                        pltpu.VMEM((2, r, 64), F32)],
        compiler_params=_cparams("parallel", "arbitrary"),
        name="diff_prompt",
    )(qb, b_new.reshape(bsz, t, 128), b_lam, g_sub, ald, al0, sc)


def _mla_prompt_body(q_ref, k_ref, v_ref, cm_ref, o_ref, v16, m_ref, l_ref, acc_ref):
    c = pl.program_id(1)

    @pl.when(c == 0)
    def _():
        for h in range(C_HEADS):
            v16[h] = v_ref[0, :, h * C_V:(h + 1) * C_V]

    start = pl.multiple_of(c * QB, QB)
    for h in range(C_HEADS):
        hs = slice(h * CP, (h + 1) * CP)
        qs = q_ref[:, hs]
        s = _dot_nt(qs, k_ref[0, pl.ds(start, QB), hs]) + cm_ref[...]
        _flash_first(s, v16[h, pl.ds(start, QB), :], m_ref, l_ref, acc_ref, h)

        def body(n, carry, h=h, hs=hs, qs=qs):
            off = pl.multiple_of(n * QB, QB)
            s = _dot_nt(qs, k_ref[0, pl.ds(off, QB), hs])
            _flash_next(s, v16[h, pl.ds(off, QB), :], m_ref, l_ref, acc_ref, h)
            return carry

        lax.fori_loop(0, c, body, 0)
        o_ref[:, h * C_V:(h + 1) * C_V] = (acc_ref[h] / l_ref[h]).astype(BF16)


def _mla_prompt(qc, kc, vc, consts, bsz, t):
    nq = t // QB
    cm = consts["c_cm"]
    return pl.pallas_call(
        _mla_prompt_body,
        grid=(bsz, nq),
        in_specs=[pl.BlockSpec((QB, CW), lambda b, c: (b * nq + c, 0)),
                  pl.BlockSpec((1, t, CW), lambda b, c: (b, 0, 0)),
                  pl.BlockSpec((1, t, 256), lambda b, c: (b, 0, 0)),
                  _resident(cm.shape)],
        out_specs=pl.BlockSpec((QB, 256), lambda b, c: (b * nq + c, 0)),
        out_shape=jax.ShapeDtypeStruct((bsz * t, 256), BF16),
        scratch_shapes=[pltpu.VMEM((C_HEADS, t, C_V), BF16),
                        pltpu.VMEM((C_HEADS, QB, 1), F32), pltpu.VMEM((C_HEADS, QB, 1), F32),
                        pltpu.VMEM((C_HEADS, QB, C_V), F32)],
        compiler_params=_cparams("parallel", "arbitrary"),
        name="mla_prompt",
    )(qc, kc.reshape(bsz, t, CW), vc.reshape(bsz, t, 256), cm)


def _pool_body(u_ref, prev_ref, cnt_ref, pw_ref, ps_ref, y_ref, ns_ref, buf, *, sb, t):
    u = u_ref[...]
    buf[:, 0:1, :] = jnp.zeros((sb, 1, D_POOL), F32)
    buf[:, 1:16, :] = prev_ref[...]
    buf[:, 16:16 + t, :] = u
    sums = {}
    run = u
    for k in range(1, 16):
        run = run + buf[:, 16 - k:16 - k + t, :]
        if k + 1 in POOL_WINDOWS:
            sums[k + 1] = run
    lane = lax.broadcasted_iota(jnp.int32, (sb, t, D_POOL), 2)
    tot = jnp.where(lane < 64, sums[2],
                    jnp.where(lane < 128, sums[4], jnp.where(lane < 192, sums[8], sums[16])))
    d = (tot / cnt_ref[...][None] - u).astype(BF16)
    y = _dot(d.reshape(sb * t, D_POOL), pw_ref[...]) * ps_ref[...]
    y_ref[...] = y.astype(BF16)
    ns_ref[...] = buf[:, t + 1:t + 16, :]


def _pool(zs, prev, cnt, pw, ps, sb, t):
    n = zs.shape[0]
    nseq = n // t
    return pl.pallas_call(
        functools.partial(_pool_body, sb=sb, t=t),
        grid=(nseq // sb,),
        in_specs=[pl.BlockSpec((sb, t, D_POOL), lambda i: (i, 0, O_DU // D_POOL)),
                  pl.BlockSpec((sb, POOL_STATE, D_POOL), lambda i: (i, 0, 0)),
                  _resident(cnt.shape), _resident(pw.shape), _resident(ps.shape)],
        out_specs=[pl.BlockSpec((sb * t, D_POOL), lambda i: (i, 0)),
                   pl.BlockSpec((sb, POOL_STATE, D_POOL), lambda i: (i, 0, 0))],
        out_shape=[jax.ShapeDtypeStruct((n, D_POOL), BF16),
                   jax.ShapeDtypeStruct((nseq, POOL_STATE, D_POOL), F32)],
        scratch_shapes=[pltpu.VMEM((sb, t + 16, D_POOL), F32)],
        compiler_params=_cparams("parallel"),
        name="pool",
    )(zs.reshape(nseq, t, ZS), prev, cnt, pw, ps)


def _merge_body(x_ref, a_ref, b_ref, c_ref, d_ref, gl_ref, wb_ref, wo_ref, o_ref):
    merged = None
    for n, r in enumerate((a_ref, b_ref, c_ref, d_ref)):
        gate = jax.nn.sigmoid(gl_ref[:, n * D_MODEL:(n + 1) * D_MODEL])
        term = gate * _dot(r[...], wb_ref[n])
        merged = term if merged is None else merged + term
    o_ref[...] = x_ref[...] + _dot(merged.astype(BF16), wo_ref[...])


def _merge(x, oa, ob, oc, od, gl, wb, wo, tm):
    n = x.shape[0]
    br = pl.BlockSpec((tm, 256), lambda i: (i, 0))
    return pl.pallas_call(
        _merge_body,
        grid=(n // tm,),
        in_specs=[pl.BlockSpec((tm, D_MODEL), lambda i: (i, 0)), br, br, br, br,
                  pl.BlockSpec((tm, 4 * D_MODEL), lambda i: (i, 0)),
                  _resident(wb.shape), _resident(wo.shape)],
        out_specs=pl.BlockSpec((tm, D_MODEL), lambda i: (i, 0)),
        out_shape=jax.ShapeDtypeStruct((n, D_MODEL), F32),
        compiler_params=_cparams("parallel"),
        name="merge",
    )(x, oa, ob, oc, od, gl, wb, wo)


FC = 256
NF = D_FF // FC


def _ffn_body(h_ref, st_ref, g_ref, wup_ref, wdn_ref, cw_ref, cb_ref, o_ref, ns_ref,
              buf, carry, *, sb, t, tiles_per_seq):
    i = pl.program_id(0)
    first = (i % tiles_per_seq) == 0
    h = h_ref[...]
    hn = (h * lax.rsqrt(jnp.mean(h * h, axis=-1, keepdims=True) + EPS) * g_ref[...]).astype(BF16)
    acc = jnp.zeros((sb * t, D_MODEL), F32)
    for f in range(NF):
        sl = slice(f * FC, (f + 1) * FC)
        ga = _dot(hn, wup_ref[:, sl]).reshape(sb, t, FC)
        va = _dot(hn, wup_ref[:, D_FF + f * FC:D_FF + (f + 1) * FC]).reshape(sb, t, FC)
        @pl.when(first)
        def _(sl=sl):
            buf[:, 6:8, :] = st_ref[:, :, sl]

        @pl.when(jnp.logical_not(first))
        def _(f=f):
            buf[:, 6:8, :] = carry[f]

        buf[:, 8:8 + t, :] = ga
        conv = (cb_ref[:, sl][None] + buf[:, 6:6 + t, :] * cw_ref[0:1, sl][None]
                + buf[:, 7:7 + t, :] * cw_ref[1:2, sl][None] + ga * cw_ref[2:3, sl][None])
        act = (conv * jax.nn.sigmoid(conv) * va).reshape(sb * t, FC).astype(BF16)
        acc = acc + _dot(act, wdn_ref[sl, :])
        last = buf[:, 6 + t:8 + t, :]
        carry[f] = last
        ns_ref[:, :, sl] = last
    o_ref[...] = h + acc


def _ffn(h, st, g, wup, wdn, cw, cb, sb, t, tiles_per_seq):
    n = h.shape[0]
    tm = sb * t
    nst = st.shape[0]
    return pl.pallas_call(
        functools.partial(_ffn_body, sb=sb, t=t, tiles_per_seq=tiles_per_seq),
        grid=(n // tm,),
        in_specs=[pl.BlockSpec((tm, D_MODEL), lambda i: (i, 0)),
                  pl.BlockSpec((sb, 2, D_FF), lambda i: (i // tiles_per_seq, 0, 0)),
                  _resident((1, D_MODEL)), _resident(wup.shape), _resident(wdn.shape),
                  _resident(cw.shape), _resident(cb.shape)],
        out_specs=[pl.BlockSpec((tm, D_MODEL), lambda i: (i, 0)),
                   pl.BlockSpec((sb, 2, D_FF), lambda i: (i // tiles_per_seq, 0, 0))],
        out_shape=[jax.ShapeDtypeStruct((n, D_MODEL), F32),
                   jax.ShapeDtypeStruct((nst, 2, D_FF), F32)],
        scratch_shapes=[pltpu.VMEM((sb, t + 8, FC), F32), pltpu.VMEM((NF, sb, 2, FC), F32)],
        compiler_params=_cparams("arbitrary"),
        name="ffn",
    )(h, st, g, wup, wdn, cw, cb)


def _page_specs(cache, layer, gs, pp):
    tail = cache.shape[2:]
    zeros = (0,) * len(tail)
    return [pl.BlockSpec((None, None) + tail,
                         lambda b, g, pt, s=s, k=k: (layer, pt[b * gs + s, g * pp + k]) + zeros)
            for s in range(gs) for k in range(pp)]


def _sres(shape):
    nd = len(shape)
    return pl.BlockSpec(shape, lambda b, g, pt: (0,) * nd, pipeline_mode=pl.Buffered(1))


def _pad_rows(x, n):
    return jnp.concatenate([x, jnp.zeros((n - x.shape[0], x.shape[1]), x.dtype)], axis=0)


def _moba_sample_body(pt_ref, q_ref, new_ref, place_ref, als_ref, ald_ref, sc_ref, *rest,
                      gs, pp, ts, past_len, nblk):
    pages = rest[:gs * pp]
    o_ref = rest[gs * pp]
    qbd, m_ref, l_ref, acc_ref, ks_ref = rest[gs * pp + 1:]
    g = pl.program_id(1)
    rows = A_HEADS * ts
    hpk = A_HEADS // A_KV
    ppb = MOBA_BLOCK // PAGE
    bps = pp // ppb
    vk = A_KV * PAGE
    lane = lax.broadcasted_iota(jnp.int32, (rows, 128), 1)

    @pl.when(g == 0)
    def _():
        for s in range(gs):
            q = q_ref[s * ts:(s + 1) * ts, :]
            for h in range(A_HEADS):
                qbd[s, h * ts:(h + 1) * ts, :] = _dot(q, place_ref[h]).astype(BF16)
        m_ref[...] = jnp.zeros(m_ref.shape, F32)
        l_ref[...] = jnp.zeros(l_ref.shape, F32)

    for s in range(gs):
        qv = qbd[s]
        mm = m_ref[s]
        ll = l_ref[s]
        accs, ksums = [], []
        for nl in range(bps):
            blk = g * bps + nl
            r32 = [pages[s * pp + nl * ppb + e][...] for e in range(ppb)]
            r16 = [r.astype(BF16) for r in r32]
            ss = []
            for e in range(ppb):
                far = (past_len - (g * pp + nl * ppb + e) * PAGE).astype(F32)
                ss.append(_dot_nt(qv, r16[e]) + als_ref[...] - sc_ref[...] * far)
            sc = jnp.concatenate(ss, axis=-1)
            m = jnp.max(sc, axis=-1, keepdims=True)
            p = jnp.exp(sc - m).astype(BF16)
            lsum = jnp.sum(p.astype(F32), axis=-1, keepdims=True)
            acc = _dot(p[:, 0:vk], r16[0])
            for e in range(1, ppb):
                acc = acc + _dot(p[:, e * vk:(e + 1) * vk], r16[e])
            accs.append(acc)
            mm = jnp.where(lane == blk, m, mm)
            ll = jnp.where(lane == blk, lsum, ll)
            ksum = jnp.sum(r32[0].reshape(vk // 8, 8, 128), axis=0)
            for e in range(1, ppb):
                ksum = ksum + jnp.sum(r32[e].reshape(vk // 8, 8, 128), axis=0)
            ksums.append(ksum)
        acc_ref[s, pl.ds(g * bps, bps)] = jnp.stack(accs)
        ks_ref[s, pl.ds(g * bps, bps)] = jnp.stack(ksums)
        m_ref[s] = mm
        l_ref[s] = ll

    @pl.when(g == pl.num_programs(1) - 1)
    def _():
        sub = lax.broadcasted_iota(jnp.int32, (nblk, 8, 128), 1)
        row_kv = lax.broadcasted_iota(jnp.int32, (rows, 128), 0) // (ts * hpk)
        for s in range(gs):
            qv = qbd[s]
            ks = ks_ref[s]
            bs = jnp.zeros((rows, 128), F32)
            for j in range(A_KV):
                kmean = jnp.sum(jnp.where(sub % A_KV == j, ks, 0.0), axis=1) * (1.0 / MOBA_BLOCK)
                hi, lo = _split16(_pad_rows(kmean, 128))
                bs = jnp.where(row_kv == j, _dot_nt(qv, hi) + _dot_nt(qv, lo), bs)
            rank = jnp.zeros((rows, 128), F32)
            for mblk in range(nblk):
                col = bs[:, mblk:mblk + 1]
                rank = rank + jnp.where(col > bs, 1.0, jnp.where((col == bs) & (lane > mblk), 1.0, 0.0))
            sel = (lane < nblk) & (rank < MOBA_TOPK)
            new16 = _pad_rows(new_ref[s * ts * A_KV:(s + 1) * ts * A_KV, :], PAGE).astype(BF16)
            s_own = _dot_nt(qv, new16) + ald_ref[...]
            m_own = jnp.max(s_own, axis=-1, keepdims=True)
            p_own = jnp.exp(s_own - m_own).astype(BF16)
            mf = jnp.maximum(m_own, jnp.max(jnp.where(sel, m_ref[s], NEG), axis=-1, keepdims=True))
            wgt = jnp.where(sel, jnp.exp(m_ref[s] - mf), 0.0)
            w_own = jnp.exp(m_own - mf)
            lsum = (w_own * jnp.sum(p_own.astype(F32), axis=-1, keepdims=True)
                    + jnp.sum(wgt * l_ref[s], axis=-1, keepdims=True))
            acc = w_own * _dot(p_own, new16)
            for n in range(nblk):
                acc = acc + wgt[:, n:n + 1] * acc_ref[s, n]
            o = acc / lsum
            for h in range(A_HEADS):
                o_ref[s * ts:(s + 1) * ts, h * 64:(h + 1) * 64] = (
                    o[h * ts:(h + 1) * ts, 64:128].astype(BF16))


def _sample_grid(bs, n_pages, gs, pp, q_spec, new_spec, consts_specs, cache, layer, scratch):
    return pltpu.PrefetchScalarGridSpec(
        num_scalar_prefetch=1,
        grid=(bs // gs, n_pages // pp),
        in_specs=[pl.BlockSpec(q_spec, lambda b, g, pt: (b, 0)),
                  pl.BlockSpec(new_spec, lambda b, g, pt: (b, 0))]
        + consts_specs + _page_specs(cache, layer, gs, pp),
        out_specs=pl.BlockSpec((q_spec[0], 256), lambda b, g, pt: (b, 0)),
        scratch_shapes=scratch,
    )


def _moba_sample(qa, a_new, cache, layer, page_table, consts, bs, ts, gs, pp):
    n_pages = page_table.shape[1]
    past_len = n_pages * PAGE
    nblk = past_len // MOBA_BLOCK
    assert nblk <= 128 and A_KV * ts <= PAGE
    rows = A_HEADS * ts
    cs = [consts[k] for k in ("sa_place", "sa_als", "sa_ald", "sa_sc")]
    cache2 = cache.reshape(cache.shape[0], cache.shape[1], A_KV * PAGE, 128)
    grid_spec = _sample_grid(
        bs, n_pages, gs, pp, (gs * ts, 256), (gs * ts * A_KV, 128), [_sres(c.shape) for c in cs],
        cache2, layer,
        [pltpu.VMEM((gs, rows, 128), BF16), pltpu.VMEM((gs, rows, 128), F32),
         pltpu.VMEM((gs, rows, 128), F32), pltpu.VMEM((gs, nblk, rows, 128), F32),
         pltpu.VMEM((gs, nblk, 8, 128), F32)])
    return pl.pallas_call(
        functools.partial(_moba_sample_body, gs=gs, pp=pp, ts=ts, past_len=past_len, nblk=nblk),
        grid_spec=grid_spec,
        out_shape=jax.ShapeDtypeStruct((bs * ts, 256), BF16),
        compiler_params=_cparams("parallel", "arbitrary"),
        name="moba_sample",
    )(page_table, qa, a_new.reshape(bs * ts * A_KV, 128), *cs, *([cache2] * (gs * pp)))


def _softmax_step(s, pv_fn, m_ref, l_ref, acc_ref, idx):
    m_prev = m_ref[idx]
    m_new = jnp.maximum(m_prev, jnp.max(s, axis=-1, keepdims=True))
    alpha = jnp.exp(m_prev - m_new)
    p = jnp.exp(s - m_new).astype(BF16)
    m_ref[idx] = m_new
    l_ref[idx] = alpha * l_ref[idx] + jnp.sum(p.astype(F32), axis=-1, keepdims=True)
    acc_ref[idx] = alpha * acc_ref[idx] + pv_fn(p)


def _pv_pages(p, r16):
    pv = _dot(p[:, 0:PAGE], r16[0])
    for k in range(1, len(r16)):
        pv = pv + _dot(p[:, k * PAGE:(k + 1) * PAGE], r16[k])
    return pv


def _diff_sample_body(pt_ref, q_ref, new_ref, place_ref, lam_ref, gs_ref, als_ref, ald_ref, sc_ref,
                      *rest, gs, pp, ts, past_len, lam_init):
    pages = rest[:gs * pp]
    o_ref = rest[gs * pp]
    qbd, m_ref, l_ref, acc_ref = rest[gs * pp + 1:]
    g = pl.program_id(1)
    half = B_HEADS * ts

    @pl.when(g == 0)
    def _():
        for s in range(gs):
            q = q_ref[s * ts:(s + 1) * ts, :]
            for e in range(2 * B_HEADS):
                qbd[s, e * ts:(e + 1) * ts, :] = _dot(q, place_ref[e]).astype(BF16)
        m_ref[...] = jnp.full(m_ref.shape, NEG, F32)
        l_ref[...] = jnp.zeros(l_ref.shape, F32)
        acc_ref[...] = jnp.zeros(acc_ref.shape, F32)

    for s in range(gs):
        qv = qbd[s]
        r16 = [pages[s * pp + k][...].astype(BF16) for k in range(pp)]
        ss = []
        for k in range(pp):
            far = (past_len - (g * pp + k) * PAGE).astype(F32)
            ss.append(_dot_nt(qv, r16[k]) + als_ref[...] - sc_ref[...] * far)
        _softmax_step(jnp.concatenate(ss, axis=-1), functools.partial(_pv_pages, r16=r16),
                      m_ref, l_ref, acc_ref, s)

    @pl.when(g == pl.num_programs(1) - 1)
    def _():
        lam = _lam_of(lam_ref[...], lam_init)
        for s in range(gs):
            new16 = _pad_rows(new_ref[s * ts:(s + 1) * ts, :], PAGE).astype(BF16)
            s_own = _dot_nt(qbd[s], new16) + ald_ref[...]
            _softmax_step(s_own, lambda p, new16=new16: _dot(p, new16), m_ref, l_ref, acc_ref, s)
            o = acc_ref[s] / l_ref[s]
            o = o[0:half, 64:128] - lam * o[half:2 * half, 64:128]
            o = (o * lax.rsqrt(jnp.mean(o * o, axis=-1, keepdims=True) + EPS) * gs_ref[...]
                 * (1.0 - lam_init))
            for h in range(B_HEADS):
                o_ref[s * ts:(s + 1) * ts, h * 64:(h + 1) * 64] = o[h * ts:(h + 1) * ts].astype(BF16)


def _diff_sample(qb, b_new, b_lam, g_sub, cache, layer, page_table, consts, bs, ts, gs, pp, lam_init):
    n_pages = page_table.shape[1]
    past_len = n_pages * PAGE
    rows = 2 * B_HEADS * ts
    cs = [consts["sb_place"], b_lam, g_sub, consts["sb_als"], consts["sb_ald"], consts["sb_sc"]]
    cache2 = cache.reshape(cache.shape[0], cache.shape[1], PAGE, 128)
    grid_spec = _sample_grid(
        bs, n_pages, gs, pp, (gs * ts, 256), (gs * ts, 128), [_sres(c.shape) for c in cs], cache2, layer,
        [pltpu.VMEM((gs, rows, 128), BF16), pltpu.VMEM((gs, rows, 1), F32),
         pltpu.VMEM((gs, rows, 1), F32), pltpu.VMEM((gs, rows, 128), F32)])
    return pl.pallas_call(
        functools.partial(_diff_sample_body, gs=gs, pp=pp, ts=ts, past_len=past_len,
                          lam_init=lam_init),
        grid_spec=grid_spec,
        out_shape=jax.ShapeDtypeStruct((bs * ts, 256), BF16),
        compiler_params=_cparams("parallel", "arbitrary"),
        name="diff_sample",
    )(page_table, qb, b_new, *cs, *([cache2] * (gs * pp)))


def _mla_sample_body(pt_ref, q_ref, new_ref, mq_ref, wext_ref, wuv_ref, cm_ref, *rest, gs, pp, ts):
    pages = rest[:gs * pp]
    o_ref = rest[gs * pp]
    lhs, pbuf, nbuf, m_ref, l_ref, acc_ref = rest[gs * pp + 1:]
    b = pl.program_id(0)
    g = pl.program_id(1)
    rows = C_HEADS * ts
    next_rows = C_HEADS * C_NOPE + C_ROPE

    @pl.when((b == 0) & (g == 0))
    def _():
        pbuf[...] = jnp.zeros(pbuf.shape, BF16)
        nbuf[...] = jnp.zeros(nbuf.shape, BF16)
        for s in range(gs):
            lhs[s, rows:rows + next_rows, :] = wext_ref[...]

    @pl.when(g == 0)
    def _():
        for s in range(gs):
            q = q_ref[s * ts:(s + 1) * ts, :]
            for h in range(C_HEADS):
                lhs[s, h * ts:(h + 1) * ts, :] = _dot(q, mq_ref[h]).astype(BF16)
        m_ref[...] = jnp.full(m_ref.shape, NEG, F32)
        l_ref[...] = jnp.zeros(l_ref.shape, F32)
        acc_ref[...] = jnp.zeros(acc_ref.shape, F32)

    def scores(s, keys16):
        out = _dot_nt(lhs[s], keys16)
        kt = out[rows:rows + next_rows]
        kt2 = kt * kt
        ssq = jnp.sum(kt2[0:C_HEADS * C_NOPE].reshape(C_HEADS, C_NOPE, PAGE), axis=1)
        ssq = ssq + jnp.sum(kt2[C_HEADS * C_NOPE:], axis=0, keepdims=True)
        r = lax.rsqrt(ssq * (1.0 / C_QK) + EPS)
        r_rows = jnp.broadcast_to(r[:, None, :], (C_HEADS, ts, PAGE)).reshape(rows, PAGE)
        return out[0:rows] * r_rows

    for s in range(gs):
        ss = []
        for k in range(pp):
            pbuf[s, k, :, 0:160] = pages[s * pp + k][...].astype(BF16)
            ss.append(scores(s, pbuf[s, k]))
        _softmax_step(jnp.concatenate(ss, axis=-1),
                      lambda p, s=s: _pv_pages(p, [pbuf[s, k] for k in range(pp)]),
                      m_ref, l_ref, acc_ref, s)

    @pl.when(g == pl.num_programs(1) - 1)
    def _():
        for s in range(gs):
            nbuf[0:ts, 0:160] = new_ref[s * ts:(s + 1) * ts, :].astype(BF16)
            nk = nbuf[...]
            _softmax_step(scores(s, nk) + cm_ref[...], lambda p, nk=nk: _dot(p, nk),
                          m_ref, l_ref, acc_ref, s)
            o_lat = (acc_ref[s][:, 0:C_KVL] / l_ref[s]).astype(BF16)
            for h in range(C_HEADS):
                o_ref[s * ts:(s + 1) * ts, h * C_V:(h + 1) * C_V] = _dot(
                    o_lat[h * ts:(h + 1) * ts], wuv_ref[:, h * C_V:(h + 1) * C_V]).astype(BF16)


def _mla_sample(qce, c_new, mq, wext, wuv, cache, layer, page_table, consts, bs, ts, gs, pp):
    n_pages = page_table.shape[1]
    rows = C_HEADS * ts
    cs = [mq, wext, wuv, consts["sc_cm"]]
    grid_spec = _sample_grid(
        bs, n_pages, gs, pp, (gs * ts, CW), (gs * ts, 160), [_sres(c.shape) for c in cs], cache, layer,
        [pltpu.VMEM((gs, rows + wext.shape[0], 256), BF16),
         pltpu.VMEM((gs, pp, PAGE, 256), BF16), pltpu.VMEM((PAGE, 256), BF16),
         pltpu.VMEM((gs, rows, 1), F32), pltpu.VMEM((gs, rows, 1), F32),
         pltpu.VMEM((gs, rows, 256), F32)])
    return pl.pallas_call(
        functools.partial(_mla_sample_body, gs=gs, pp=pp, ts=ts),
        grid_spec=grid_spec,
        out_shape=jax.ShapeDtypeStruct((bs * ts, 256), BF16),
        compiler_params=_cparams("arbitrary", "arbitrary"),
        name="mla_sample",
    )(page_table, qce, c_new, *cs, *([cache] * (gs * pp)))


def _alibi(n):
    return np.asarray(2.0 ** (-8.0 * np.arange(1, n + 1) / n), dtype=np.float32)


def _group_matrix(width, group):
    idx = np.arange(width) // group
    return jnp.asarray(idx[:, None] == idx[None, :], dtype=BF16)


def _rope_tables(pos, reps):
    half = C_ROPE // 2
    inv = ROPE_THETA ** (-jnp.arange(half, dtype=F32) / half)
    ang = pos.astype(F32)[:, None] * inv[None, :]
    cos, sin = jnp.cos(ang), jnp.sin(ang)
    n = pos.shape[0]
    one, zero = jnp.ones((n, C_NOPE), F32), jnp.zeros((n, C_NOPE), F32)
    zh = jnp.zeros((n, half), F32)
    zp = jnp.zeros((n, CP - C_QK), F32)
    qcos = jnp.concatenate([one, cos, cos, zp] * C_HEADS, axis=-1)
    qsa = jnp.concatenate([zero, -sin, zh, zp] * C_HEADS, axis=-1)
    qsb = jnp.concatenate([zero, zh, sin, zp] * C_HEADS, axis=-1)
    pad = jnp.zeros((n, 128 - C_ROPE), F32)
    kcos = jnp.concatenate([cos, cos, pad], axis=-1)
    ksa = jnp.concatenate([-sin, zh, pad], axis=-1)
    ksb = jnp.concatenate([zh, sin, pad], axis=-1)
    rt = jnp.concatenate([qcos, qsa, qsb, kcos, ksa, ksb], axis=-1)
    return jnp.tile(rt, (reps, 1))


def _pool_counts(pos):
    win = np.repeat(np.asarray(POOL_WINDOWS), POOL_GROUP)[None, :]
    return jnp.asarray(np.minimum(win, np.asarray(pos)[:, None] + 1), dtype=F32)


def _make_consts(tp, ts, past_len, tm_s):
    c = {}
    sl = _alibi(A_HEADS)
    i = np.arange(QB)[:, None]
    r = np.arange(QB)[None, :]
    dist = (i - r).astype(np.float32)
    causal = i >= r

    def alibi_tables(heads):
        al0 = np.concatenate([-sl[h] * dist for h in heads], axis=0)
        ald = np.concatenate([np.where(causal, -sl[h] * dist, NEG) for h in heads], axis=0)
        sc = np.concatenate([np.full((QB, 1), sl[h], np.float32) for h in heads], axis=0)
        return ald.astype(np.float32), al0.astype(np.float32), sc

    hpk = A_HEADS // A_KV
    per = [alibi_tables(range(j * hpk, (j + 1) * hpk)) for j in range(A_KV)]
    c["a_ald"], c["a_al0"], c["a_sc"] = (jnp.asarray(np.stack([p[k] for p in per])) for k in range(3))
    c["b_ald"], c["b_al0"], c["b_sc"] = (jnp.asarray(x) for x in alibi_tables(range(B_HEADS)))
    c["c_cm"] = jnp.asarray(np.where(causal, 0.0, NEG).astype(np.float32))
    nb = tp // MOBA_BLOCK
    assert nb + MOBA_TOPK <= 32 and tp % QB == 0
    t_idx = np.arange(tp)[:, None] // MOBA_BLOCK
    lane = np.arange(128)[None, :]
    c["a_avg"] = jnp.asarray(np.where(t_idx == lane % 32, 1.0 / MOBA_BLOCK, 0.0), dtype=BF16)
    c["rt_p"] = _rope_tables(jnp.arange(tp, dtype=jnp.int32), 1)
    c["rt_s"] = _rope_tables(past_len + jnp.arange(ts, dtype=jnp.int32), tm_s // ts)
    c["cnt_p"] = _pool_counts(np.arange(tp))
    c["cnt_s"] = _pool_counts(past_len + np.arange(ts))
    c["g64"], c["g32"], c["g96"] = _group_matrix(256, 64), _group_matrix(256, 32), _group_matrix(CW, CP)

    assert past_len % MOBA_BLOCK == 0 and ts <= PAGE and MOBA_BLOCK % PAGE == 0
    qi = np.tile(np.arange(ts), A_HEADS)[:, None]
    kr = np.arange(PAGE)[None, :]
    slr = np.repeat(sl, ts)[:, None]
    als = (-slr * (qi - kr)).astype(np.float32)
    own_ok = (kr <= qi) & (kr < ts)
    ald = np.where(own_ok, als, NEG).astype(np.float32)
    c["sb_als"], c["sb_ald"], c["sb_sc"] = (jnp.asarray(np.concatenate([x, x], axis=0))
                                            for x in (als, ald, slr))
    c["sc_cm"] = jnp.asarray(np.where(own_ok, 0.0, NEG).astype(np.float32))
    vr = np.arange(A_KV * PAGE)[None, :]
    row_kv = (np.repeat(np.arange(A_HEADS), ts) // hpk)[:, None]
    mine = (vr % A_KV) == row_kv
    als_a = np.where(mine, -slr * (qi - vr // A_KV), NEG).astype(np.float32)
    vo = np.arange(PAGE)[None, :]
    own_a = ((vo % A_KV) == row_kv) & (vo // A_KV <= qi) & (vo // A_KV < ts)
    ald_a = np.where(own_a, -slr * (qi - vo // A_KV), NEG).astype(np.float32)
    c["sa_als"], c["sa_ald"], c["sa_sc"] = jnp.asarray(als_a), jnp.asarray(ald_a), jnp.asarray(slr)
    pa = np.zeros((A_HEADS, 256, 128), np.float32)
    for h in range(A_HEADS):
        pa[h, h * 64 + np.arange(64), np.arange(64)] = 1.0
    c["sa_place"] = jnp.asarray(pa, dtype=BF16)
    pb = np.zeros((2 * B_HEADS, 256, 128), np.float32)
    for e in range(2):
        for h in range(B_HEADS):
            pb[e * B_HEADS + h, h * 64 + e * 32 + np.arange(32), e * 32 + np.arange(32)] = 1.0
    c["sb_place"] = jnp.asarray(pb, dtype=BF16)
    return c


def _layer_weights(l, p):
    w = {}
    w_in = p["w_in"][l]
    sizes = (256, 128, 128, 256, 64, 64, 256, 160, 256, 4096)
    offs = np.concatenate([[0], np.cumsum(sizes)])
    aq, ak, av, bq, bk, bv, cq, ckv, du, gl = (w_in[:, offs[k]:offs[k + 1]] for k in range(10))
    akv = jnp.concatenate([ak[:, 0:64], av[:, 0:64], ak[:, 64:128], av[:, 64:128]], axis=1)
    pad = jnp.zeros((D_MODEL, 96), F32)
    w["ws"] = jnp.concatenate([aq, akv, bq, cq, du, ckv, pad, bk, bv], axis=1).astype(BF16)
    w["wg"] = gl.astype(BF16)
    w["g1"] = p["norm1_g"][l][None]
    w["g2"] = p["norm2_g"][l][None]

    def row(v):
        return jnp.pad(v, (0, CW - v.shape[0]))[None]

    def heads_padded(v):
        return jnp.tile(jnp.pad(v, (0, CP - C_QK)), C_HEADS)

    one64 = jnp.ones((64,), F32)
    zero64 = jnp.zeros((64,), F32)
    gains = [
        row(jnp.tile(p["a_q_g"][l], 4) * (A_DIM ** -0.5)),
        row(jnp.concatenate([p["a_k_g"][l], one64] * 2)),
        row(jnp.concatenate([one64, zero64] * 2)),
        row(jnp.tile(p["b_q_g"][l], 8) * (B_QK ** -0.5)),
        row(jnp.concatenate([p["b_k_g"][l], p["b_k_g"][l], one64])),
        row(jnp.concatenate([one64, zero64])),
        row(p["c_qa_g"][l]),
        row(heads_padded(p["c_q_g"][l]) * (C_QK ** -0.5)),
        row(p["c_kva_g"][l]),
        row(heads_padded(p["c_k_g"][l])),
    ]
    w["gains"] = jnp.concatenate(gains + [jnp.zeros((16 - len(gains), CW), F32)], axis=0)
    w["wuq"] = jnp.pad(p["w_uq"][l], ((0, 0), (0, 0), (0, CP - C_QK))).reshape(C_QL, CW).astype(BF16)
    w_uk = p["w_uk"][l]
    eye = jnp.eye(C_ROPE, dtype=F32)
    top = jnp.concatenate([jnp.concatenate([w_uk[:, h], jnp.zeros((C_KVL, CP - C_NOPE), F32)], axis=1)
                           for h in range(C_HEADS)], axis=1)
    mid = jnp.concatenate([jnp.concatenate([jnp.zeros((C_ROPE, C_NOPE), F32), eye,
                                            jnp.zeros((C_ROPE, CP - C_QK), F32)], axis=1)
                           for _ in range(C_HEADS)], axis=1)
    w["wkc"] = jnp.concatenate([top, mid, jnp.zeros((96, CW), F32)], axis=0).astype(BF16)
    w["wuv"] = p["w_uv"][l].reshape(C_KVL, C_HEADS * C_V).astype(BF16)
    mq = []
    for h in range(C_HEADS):
        m = jnp.zeros((CW, 256), F32)
        m = m.at[h * CP:h * CP + C_NOPE, 0:C_KVL].set(w_uk[:, h].T)
        m = m.at[h * CP + C_NOPE:h * CP + C_QK, C_KVL:C_KVL + C_ROPE].set(eye)
        mq.append(m)
    w["mq"] = jnp.stack(mq).astype(BF16)
    probe = jnp.concatenate([w_uk[:, h].T for h in range(C_HEADS)], axis=0)
    probe = jnp.concatenate([probe, jnp.zeros((C_HEADS * C_NOPE, 128), F32)], axis=1)
    rope_rows = jnp.zeros((C_ROPE, 256), F32).at[:, C_KVL:C_KVL + C_ROPE].set(eye)
    w["wext"] = jnp.concatenate([probe, rope_rows], axis=0).astype(BF16)
    pw = jnp.zeros((D_POOL, D_POOL), F32)
    for gi in range(len(POOL_WINDOWS)):
        sl = slice(gi * POOL_GROUP, (gi + 1) * POOL_GROUP)
        pw = pw.at[sl, sl].set(p["pool_w"][l][gi])
    w["pw"] = pw.astype(BF16)
    w["ps"] = p["pool_scale"][l][None]
    w["wb"] = p["w_branch"][l].astype(BF16)
    w["wo"] = p["w_out"][l].astype(BF16)
    w["wup"] = p["w_up"][l].astype(BF16)
    w["wdn"] = p["w_down"][l].astype(BF16)
    w["cw"] = p["conv_w"][l]
    w["cb"] = p["conv_b"][l][None]
    w["b_lam"] = p["b_lam"][l]
    w["g_sub"] = p["b_sub_g"][l][None]
    return w


def _tile(n, pref):
    return pref if n % pref == 0 else n


def kernel(x_prompt, x_sample, cache_a, cache_b, cache_c, state_pool, state_conv, page_table,
           norm1_g, w_in, a_q_g, a_k_g, b_q_g, b_k_g, b_lam, b_sub_g, c_qa_g, c_kva_g,
           w_uq, w_uk, w_uv, c_q_g, c_k_g, pool_w, pool_scale, w_branch, w_out,
           norm2_g, w_up, conv_w, conv_b, w_down):
    params = dict(norm1_g=norm1_g, w_in=w_in, a_q_g=a_q_g, a_k_g=a_k_g, b_q_g=b_q_g, b_k_g=b_k_g,
                  b_lam=b_lam, b_sub_g=b_sub_g, c_qa_g=c_qa_g, c_kva_g=c_kva_g, w_uq=w_uq,
                  w_uk=w_uk, w_uv=w_uv, c_q_g=c_q_g, c_k_g=c_k_g, pool_w=pool_w,
                  pool_scale=pool_scale, w_branch=w_branch, w_out=w_out, norm2_g=norm2_g,
                  w_up=w_up, conv_w=conv_w, conv_b=conv_b, w_down=w_down)
    bp, tp, _ = x_prompt.shape
    bs, ts, _ = x_sample.shape
    depth = w_in.shape[0]
    n_pages = page_table.shape[1]
    past_len = n_pages * PAGE
    n_s = bs * ts
    tm_p = _tile(tp, 256)
    tm_s = _tile(n_s, 256)
    tf_p = _tile(tp, 512)
    sb_s = tm_s // ts
    pp = 16 if n_pages % 16 == 0 else n_pages
    gs = 4 if bs % 4 == 0 else 1
    consts = _make_consts(tp, ts, past_len, tm_s)
    zero_pool = jnp.zeros((bp, POOL_STATE, D_POOL), F32)
    zero_conv = jnp.zeros((bp, 2, D_FF), F32)

    xp = x_prompt.reshape(bp * tp, D_MODEL)
    xs = x_sample.reshape(n_s, D_MODEL)
    new_p = [[] for _ in range(5)]
    new_s = [[] for _ in range(5)]
    for l in range(depth):
        w = _layer_weights(l, params)
        lam_init = 0.8 - 0.6 * math.exp(-0.3 * l)
        zs, gl = _inproj(xp, w["g1"], w["ws"], w["wg"], tm_p)
        qa, a_new, qb, b_new, qc, _, c_new, kc, vc = _prep(
            zs, consts["rt_p"], w["gains"], consts["g64"], consts["g32"], consts["g96"],
            w["wuq"], w["wkc"], w["wuv"], tm_p)
        oa = _moba_prompt(qa, a_new, consts, bp, tp)
        ob = _diff_prompt(qb, b_new, w["b_lam"], w["g_sub"], consts, bp, tp, lam_init)
        oc = _mla_prompt(qc, kc, vc, consts, bp, tp)
        od, pool_new = _pool(zs, zero_pool, consts["cnt_p"], w["pw"], w["ps"], 1, tp)
        h = _merge(xp, oa, ob, oc, od, gl, w["wb"], w["wo"], tm_p)
        xp, conv_new = _ffn(h, zero_conv, w["g2"], w["wup"], w["wdn"], w["cw"], w["cb"],
                            1, tf_p, tp // tf_p)
        for k, v in enumerate((a_new.reshape(bp, tp, A_KV, 128), b_new.reshape(bp, tp, 1, 128),
                               c_new.reshape(bp, tp, 160), pool_new, conv_new)):
            new_p[k].append(v)
        zs, gl = _inproj(xs, w["g1"], w["ws"], w["wg"], tm_s)
        qa, a_new, qb, b_new, _, qce, c_new, _, _ = _prep(
            zs, consts["rt_s"], w["gains"], consts["g64"], consts["g32"], consts["g96"],
            w["wuq"], w["wkc"], w["wuv"], tm_s)
        oa = _moba_sample(qa, a_new, cache_a, l, page_table, consts, bs, ts, gs, pp)
        ob = _diff_sample(qb, b_new, w["b_lam"], w["g_sub"], cache_b, l, page_table, consts, bs, ts,
                          gs, pp, lam_init)
        oc = _mla_sample(qce, c_new, w["mq"], w["wext"], w["wuv"], cache_c, l, page_table, consts,
                         bs, ts, gs, pp)
        od, pool_new = _pool(zs, state_pool[l], consts["cnt_s"], w["pw"], w["ps"], sb_s, ts)
        h = _merge(xs, oa, ob, oc, od, gl, w["wb"], w["wo"], tm_s)
        xs, conv_new = _ffn(h, state_conv[l], w["g2"], w["wup"], w["wdn"], w["cw"], w["cb"],
                            sb_s, ts, 1)
        for k, v in enumerate((a_new.reshape(bs, ts, A_KV, 128), b_new.reshape(bs, ts, 1, 128),
                               c_new.reshape(bs, ts, 160), pool_new, conv_new)):
            new_s[k].append(v)
    np_ = [jnp.stack(v, axis=0) for v in new_p]
    ns_ = [jnp.stack(v, axis=0) for v in new_s]
    return (xp.reshape(bp, tp, D_MODEL), xs.reshape(bs, ts, D_MODEL),
            np_[0], np_[1], np_[2], np_[3], np_[4], ns_[0], ns_[1], ns_[2], ns_[3], ns_[4])
```

```python
import functools
import math

import numpy as np
import jax
import jax.numpy as jnp
from jax import lax
from jax.experimental import pallas as pl
from jax.experimental.pallas import tpu as pltpu

F32 = jnp.float32
BF16 = jnp.bfloat16

D_MODEL = 1024
PAGE = 128
A_HEADS, A_KV, A_DIM = 4, 2, 64
MOBA_BLOCK, MOBA_TOPK = 256, 3
B_HEADS, B_QK, B_V = 4, 32, 64
C_HEADS, C_QL, C_KVL, C_NOPE, C_ROPE, C_V = 4, 256, 128, 64, 32, 64
C_QK = C_NOPE + C_ROPE
CP = 128
CW = C_HEADS * CP
ROPE_THETA = 10000.0
POOL_WINDOWS = (2, 4, 8, 16)
POOL_GROUP = 64
D_POOL = 256
POOL_STATE = 15
D_FF = 2816
EPS = 1e-6
NEG = -1e30

ZS = 1664
O_AQ, O_AKV, O_BQ, O_CQ, O_DU, O_CKV, O_BKV = 0, 256, 512, 768, 1024, 1280, 1536

QB = 256
VMEM_LIMIT = 56 * 1024 * 1024


def _cparams(*sem):
    return pltpu.CompilerParams(dimension_semantics=sem, vmem_limit_bytes=VMEM_LIMIT)


def _resident(shape):
    nd = len(shape)
    return pl.BlockSpec(shape, lambda *_: (0,) * nd, pipeline_mode=pl.Buffered(1))


def _dot(a, b):
    return jnp.dot(a, b, preferred_element_type=F32)


def _dot_nt(a, b):
    return lax.dot_general(a, b, (((1,), (1,)), ((), ())), preferred_element_type=F32)


def _split16(x):
    hi = x.astype(BF16)
    lo = (x - hi.astype(F32)).astype(BF16)
    return hi, lo


def _inproj_body(x_ref, g_ref, ws_ref, wg_ref, zs_ref, gl_ref):
    x = x_ref[...]
    ms = jnp.mean(x * x, axis=-1, keepdims=True)
    xn = (x * lax.rsqrt(ms + EPS) * g_ref[...]).astype(BF16)
    zs_ref[...] = _dot(xn, ws_ref[...])
    for c in range(4):
        sl = slice(c * D_MODEL, (c + 1) * D_MODEL)
        gl_ref[:, sl] = _dot(xn, wg_ref[:, sl])


def _inproj(x, g, ws, wg, tm):
    n = x.shape[0]
    return pl.pallas_call(
        _inproj_body,
        grid=(n // tm,),
        in_specs=[pl.BlockSpec((tm, D_MODEL), lambda i: (i, 0)),
                  _resident((1, D_MODEL)), _resident(ws.shape), _resident(wg.shape)],
        out_specs=[pl.BlockSpec((tm, ZS), lambda i: (i, 0)),
                   pl.BlockSpec((tm, 4 * D_MODEL), lambda i: (i, 0))],
        out_shape=[jax.ShapeDtypeStruct((n, ZS), F32),
                   jax.ShapeDtypeStruct((n, 4 * D_MODEL), F32)],
        compiler_params=_cparams("parallel"),
        name="inproj",
    )(x, g, ws, wg)


G_AQ, G_AK, M_AK, G_BQ, G_BK, M_BK, G_CQA, G_CQ, G_CKVA, G_CK = range(10)


def _gnorm(x, gmat, n):
    ss = _dot((x * x).astype(BF16), gmat)
    return x * lax.rsqrt(ss * (1.0 / n) + EPS)


def _rope_cols(x, cos, sina, sinb):
    outs = []
    for c in range(x.shape[1] // 128):
        sl = slice(c * 128, (c + 1) * 128)
        xc = x[:, sl]
        outs.append(xc * cos[:, sl] + pltpu.roll(xc, 112, 1) * sina[:, sl]
                    + pltpu.roll(xc, 16, 1) * sinb[:, sl])
    return outs[0] if len(outs) == 1 else jnp.concatenate(outs, axis=-1)


def _prep_body(zs_ref, rt_ref, gn_ref, g64_ref, g32_ref, g96_ref, wuq_ref, wkc_ref, wuv_ref,
               qa_ref, an_ref, qb_ref, bn_ref, qc_ref, qce_ref, cn_ref, kc_ref, vc_ref):
    def gain(row, w):
        return gn_ref[row:row + 1, :w]

    aq = zs_ref[:, O_AQ:O_AQ + 256]
    qa_ref[...] = (_gnorm(aq, g64_ref[...], A_DIM) * gain(G_AQ, 256)).astype(BF16)
    akv = zs_ref[:, O_AKV:O_AKV + 256]
    an_ref[...] = jnp.where(gain(M_AK, 256) > 0.5,
                            _gnorm(akv, g64_ref[...], A_DIM) * gain(G_AK, 256), akv)
    bq = zs_ref[:, O_BQ:O_BQ + 256]
    qb_ref[...] = (_gnorm(bq, g32_ref[...], B_QK) * gain(G_BQ, 256)).astype(BF16)
    bkv = zs_ref[:, O_BKV:O_BKV + 128]
    bn_ref[...] = jnp.where(gain(M_BK, 128) > 0.5,
                            _gnorm(bkv, g32_ref[:128, :128], B_QK) * gain(G_BK, 128), bkv)
    cq = zs_ref[:, O_CQ:O_CQ + 256]
    cqn = cq * lax.rsqrt(jnp.mean(cq * cq, axis=-1, keepdims=True) + EPS) * gain(G_CQA, 256)
    qc = _dot(cqn.astype(BF16), wuq_ref[...])
    qc = _rope_cols(qc, rt_ref[:, 0:CW], rt_ref[:, CW:2 * CW], rt_ref[:, 2 * CW:3 * CW])
    qcn = _gnorm(qc, g96_ref[...], C_QK) * gain(G_CQ, CW)
    qc_ref[...] = qcn.astype(BF16)
    qce_ref[...] = (qcn * gain(G_CK, CW)).astype(BF16)
    lat = zs_ref[:, O_CKV:O_CKV + 128]
    latn = lat * lax.rsqrt(jnp.mean(lat * lat, axis=-1, keepdims=True) + EPS) * gain(G_CKVA, 128)
    kr = _rope_cols(zs_ref[:, O_CKV + 128:O_CKV + 256], rt_ref[:, 3 * CW:3 * CW + 128],
                    rt_ref[:, 3 * CW + 128:3 * CW + 256], rt_ref[:, 3 * CW + 256:3 * CW + 384])
    cn_ref[:, 0:128] = latn
    cn_ref[:, 128:160] = kr[:, 0:32]
    cfull = jnp.concatenate([latn, kr], axis=-1).astype(BF16)
    kc = _gnorm(_dot(cfull, wkc_ref[...]), g96_ref[...], C_QK) * gain(G_CK, CW)
    kc_ref[...] = kc.astype(BF16)
    vc_ref[...] = _dot(latn.astype(BF16), wuv_ref[...]).astype(BF16)


def _prep(zs, rt, gains, g64, g32, g96, wuq, wkc, wuv, tm):
    n = zs.shape[0]
    nt = rt.shape[0] // tm
    outs = [(256, BF16), (256, F32), (256, BF16), (128, F32), (CW, BF16), (CW, BF16),
            (160, F32), (CW, BF16), (256, BF16)]
    return pl.pallas_call(
        _prep_body,
        grid=(n // tm,),
        in_specs=[pl.BlockSpec((tm, ZS), lambda i: (i, 0)),
                  pl.BlockSpec((tm, rt.shape[1]), lambda i: (i % nt, 0)),
                  _resident(gains.shape), _resident(g64.shape), _resident(g32.shape),
                  _resident(g96.shape), _resident(wuq.shape), _resident(wkc.shape),
                  _resident(wuv.shape)],
        out_specs=[pl.BlockSpec((tm, w), lambda i: (i, 0)) for w, _ in outs],
        out_shape=[jax.ShapeDtypeStruct((n, w), dt) for w, dt in outs],
        compiler_params=_cparams("parallel"),
        name="prep",
    )(zs, rt, gains, g64, g32, g96, wuq, wkc, wuv)


def _tflash(streams, m_ref, l_ref, acc_ref, first):
    scores = [_dot(k_blk, qt) + bias for k_blk, qt, bias, _, _ in streams]
    parts = []
    for s, (_, _, _, _, idx) in zip(scores, streams):
        m_cur = jnp.max(s, axis=0, keepdims=True)
        if first:
            m_new, alpha = m_cur, None
        else:
            m_prev = m_ref[idx]
            m_new = jnp.maximum(m_prev, m_cur)
            alpha = jnp.exp(m_prev - m_new)
        p = jnp.exp(s - m_new)
        parts.append((m_new, alpha, jnp.sum(p, axis=0, keepdims=True), p.astype(BF16)))
    pvs = [_dot(vt_blk, p16) for (_, _, _, vt_blk, _), (_, _, _, p16) in zip(streams, parts)]
    for (_, _, _, _, idx), (m_new, alpha, psum, _), pv in zip(streams, parts, pvs):
        m_ref[idx] = m_new
        if first:
            l_ref[idx] = psum
            acc_ref[idx] = pv
        else:
            l_ref[idx] = alpha * l_ref[idx] + psum
            acc_ref[idx] = alpha * acc_ref[idx] + pv


def _lanes(pieces):
    return pieces[0] if len(pieces) == 1 else jnp.concatenate(pieces, axis=-1)


def _moba_prompt_body(q_ref, kv_ref, ald_ref, al0_ref, sc_ref, avg_ref, o_ref,
                      k16, vt16, km_ref, selt_ref, m_ref, l_ref, acc_ref, *, nb):
    c = pl.program_id(1)
    hpk = A_HEADS // A_KV

    @pl.when(c == 0)
    def _():
        kv = kv_ref[0]
        kvt = kv.T
        for j in range(A_KV):
            k16[j] = kv[:, j * 128:j * 128 + 64].astype(BF16)
            vt16[j] = kvt[j * 128 + 64:(j + 1) * 128, :].astype(BF16)
        hi, lo = _split16(kvt)
        kmr = _dot(hi, avg_ref[...]) + _dot(lo, avg_ref[...])
        lane = lax.broadcasted_iota(jnp.int32, (64, 128), 1)
        rows = []
        for h in range(A_HEADS):
            j = h // hpk
            rows.append(jnp.where(lane // 32 == h, kmr[j * 128:j * 128 + 64, :], 0.0))
        hi, lo = _split16(jnp.concatenate(rows, axis=0))
        km_ref[0] = hi
        km_ref[1] = lo

    q = q_ref[...]
    bs = _dot(q, km_ref[0]) + _dot(q, km_ref[1])
    lane = lax.broadcasted_iota(jnp.int32, (QB, 128), 1)
    valid = (lane % 32) < c
    bsm = jnp.where(valid, bs, NEG)
    rank = jnp.zeros((QB, 128), F32)
    for k in range(1, nb):
        rank = rank + jnp.where(pltpu.roll(bsm, k, 1) >= bsm, 1.0, 0.0)
        rank = rank + jnp.where(pltpu.roll(bsm, 128 - k, 1) > bsm, 1.0, 0.0)
    selt_ref[...] = jnp.where(valid, jnp.where(rank < MOBA_TOPK, 1.0, 0.0), 0.0).T

    qt = q.astype(F32).T
    start = pl.multiple_of(c * QB, QB)
    qss = [_lanes([qt[(hpk * j + e) * 64:(hpk * j + e + 1) * 64, :] for e in range(hpk)]).astype(BF16)
           for j in range(A_KV)]
    _tflash([(k16[j, pl.ds(start, QB), :], qss[j], ald_ref[j], vt16[j, :, pl.ds(start, QB)], j)
             for j in range(A_KV)], m_ref, l_ref, acc_ref, True)

    def body(n, carry):
        off = pl.multiple_of(n * QB, QB)
        far = ((c - n) * QB).astype(F32)
        streams = []
        for j in range(A_KV):
            selrow = _lanes([selt_ref[pl.ds((hpk * j + e) * 32 + n, 1), :] for e in range(hpk)])
            bias = jnp.where(selrow > 0.5, al0_ref[j] - sc_ref[j] * far, NEG)
            streams.append((k16[j, pl.ds(off, QB), :], qss[j], bias, vt16[j, :, pl.ds(off, QB)], j))
        _tflash(streams, m_ref, l_ref, acc_ref, False)
        return carry

    lax.fori_loop(0, c, body, 0)
    outs = []
    for j in range(A_KV):
        ot = acc_ref[j] / l_ref[j]
        outs += [ot[:, e * QB:(e + 1) * QB] for e in range(hpk)]
    o_ref[...] = jnp.concatenate(outs, axis=0).T.astype(BF16)


def _moba_prompt(qa, a_new, consts, bsz, t):
    nq = t // QB
    nb = t // MOBA_BLOCK
    ald, al0, sc, avg = consts["a_ald"], consts["a_al0"], consts["a_sc"], consts["a_avg"]
    r = (A_HEADS // A_KV) * QB
    return pl.pallas_call(
        functools.partial(_moba_prompt_body, nb=nb),
        grid=(bsz, nq),
        in_specs=[pl.BlockSpec((QB, 256), lambda b, c: (b * nq + c, 0)),
                  pl.BlockSpec((1, t, 256), lambda b, c: (b, 0, 0)),
                  _resident(ald.shape), _resident(al0.shape), _resident(sc.shape),
                  _resident(avg.shape)],
        out_specs=pl.BlockSpec((QB, 256), lambda b, c: (b * nq + c, 0)),
        out_shape=jax.ShapeDtypeStruct((bsz * t, 256), BF16),
        scratch_shapes=[pltpu.VMEM((A_KV, t, 64), BF16), pltpu.VMEM((A_KV, 64, t), BF16),
                        pltpu.VMEM((2, 256, 128), BF16), pltpu.VMEM((128, QB), F32),
                        pltpu.VMEM((A_KV, 1, r), F32), pltpu.VMEM((A_KV, 1, r), F32),
                        pltpu.VMEM((A_KV, 64, r), F32)],
        compiler_params=_cparams("parallel", "arbitrary"),
        name="moba_prompt",
    )(qa, a_new.reshape(bsz, t, 256), ald, al0, sc, avg)


def _lam_of(lp, lam_init):
    a = jnp.sum(lp[0:1] * lp[1:2], axis=-1, keepdims=True)
    b = jnp.sum(lp[2:3] * lp[3:4], axis=-1, keepdims=True)
    return jnp.exp(a) - jnp.exp(b) + lam_init


def _diff_prompt_body(q_ref, kv_ref, lam_ref, gs_ref, ald_ref, al0_ref, sc_ref, o_ref,
                      k16, vt16, m_ref, l_ref, acc_ref, *, lam_init):
    c = pl.program_id(1)

    @pl.when(c == 0)
    def _():
        kv = kv_ref[0]
        for i in range(2):
            k16[i] = kv[:, i * 32:(i + 1) * 32].astype(BF16)
        vt16[...] = kv.T[64:128, :].astype(BF16)

    qt = q_ref[...].astype(F32).T
    start = pl.multiple_of(c * QB, QB)
    qss = [_lanes([qt[h * 64 + i * 32:h * 64 + (i + 1) * 32, :] for h in range(B_HEADS)]).astype(BF16)
           for i in range(2)]
    _tflash([(k16[i, pl.ds(start, QB), :], qss[i], ald_ref[...], vt16[:, pl.ds(start, QB)], i)
             for i in range(2)], m_ref, l_ref, acc_ref, True)

    def body(n, carry):
        off = pl.multiple_of(n * QB, QB)
        far = ((c - n) * QB).astype(F32)
        bias = al0_ref[...] - sc_ref[...] * far
        _tflash([(k16[i, pl.ds(off, QB), :], qss[i], bias, vt16[:, pl.ds(off, QB)], i)
                 for i in range(2)], m_ref, l_ref, acc_ref, False)
        return carry

    lax.fori_loop(0, c, body, 0)

    lam = _lam_of(lam_ref[...], lam_init)
    ot = acc_ref[0] / l_ref[0] - lam * (acc_ref[1] / l_ref[1])
    ot = ot * lax.rsqrt(jnp.mean(ot * ot, axis=0, keepdims=True) + EPS) * gs_ref[...] * (1.0 - lam_init)
    stacked = jnp.concatenate([ot[:, h * QB:(h + 1) * QB] for h in range(B_HEADS)], axis=0)
    o_ref[...] = stacked.T.astype(BF16)


def _diff_prompt(qb, b_new, b_lam, g_sub, consts, bsz, t, lam_init):
    nq = t // QB
    ald, al0, sc = consts["b_ald"], consts["b_al0"], consts["b_sc"]
    r = B_HEADS * QB
    g_col = g_sub.reshape(B_V, 1)
    return pl.pallas_call(
        functools.partial(_diff_prompt_body, lam_init=lam_init),
        grid=(bsz, nq),
        in_specs=[pl.BlockSpec((QB, 256), lambda b, c: (b * nq + c, 0)),
                  pl.BlockSpec((1, t, 128), lambda b, c: (b, 0, 0)),
                  _resident(b_lam.shape), _resident(g_col.shape),
                  _resident(ald.shape), _resident(al0.shape), _resident(sc.shape)],
        out_specs=pl.BlockSpec((QB, 256), lambda b, c: (b * nq + c, 0)),
        out_shape=jax.ShapeDtypeStruct((bsz * t, 256), BF16),
        scratch_shapes=[pltpu.VMEM((2, t, 32), BF16), pltpu.VMEM((B_V, t), BF16),
                        pltpu.VMEM((2, 1, r), F32), pltpu.VMEM((2, 1, r), F32),
                        pltpu.VMEM((2, B_V, r), F32)],
        compiler_params=_cparams("parallel", "arbitrary"),
        name="diff_prompt",
    )(qb, b_new.reshape(bsz, t, 128), b_lam, g_col, ald, al0, sc)


def _mla_prompt_body(q_ref, k_ref, v_ref, cm_ref, o_ref, vt16, m_ref, l_ref, acc_ref):
    c = pl.program_id(1)

    @pl.when(c == 0)
    def _():
        vt = v_ref[0].astype(F32).T
        for h in range(C_HEADS):
            vt16[h] = vt[h * C_V:(h + 1) * C_V, :].astype(BF16)

    start = pl.multiple_of(c * QB, QB)
    hss = [slice(h * CP, (h + 1) * CP) for h in range(C_HEADS)]
    qss = [q_ref[:, hs].astype(F32).T.astype(BF16) for hs in hss]
    _tflash([(k_ref[0, pl.ds(start, QB), hss[h]], qss[h], cm_ref[...], vt16[h, :, pl.ds(start, QB)], h)
             for h in range(C_HEADS)], m_ref, l_ref, acc_ref, True)

    def body(n, carry):
        off = pl.multiple_of(n * QB, QB)
        _tflash([(k_ref[0, pl.ds(off, QB), hss[h]], qss[h], 0.0, vt16[h, :, pl.ds(off, QB)], h)
                 for h in range(C_HEADS)], m_ref, l_ref, acc_ref, False)
        return carry

    lax.fori_loop(0, c, body, 0)
    outs = [acc_ref[h] / l_ref[h] for h in range(C_HEADS)]
    o_ref[...] = jnp.concatenate(outs, axis=0).T.astype(BF16)


def _mla_prompt(qc, kc, vc, consts, bsz, t):
    nq = t // QB
    cm = consts["c_cm"]
    return pl.pallas_call(
        _mla_prompt_body,
        grid=(bsz, nq),
        in_specs=[pl.BlockSpec((QB, CW), lambda b, c: (b * nq + c, 0)),
                  pl.BlockSpec((1, t, CW), lambda b, c: (b, 0, 0)),
                  pl.BlockSpec((1, t, 256), lambda b, c: (b, 0, 0)),
                  _resident(cm.shape)],
        out_specs=pl.BlockSpec((QB, 256), lambda b, c: (b * nq + c, 0)),
        out_shape=jax.ShapeDtypeStruct((bsz * t, 256), BF16),
        scratch_shapes=[pltpu.VMEM((C_HEADS, C_V, t), BF16),
                        pltpu.VMEM((C_HEADS, 1, QB), F32), pltpu.VMEM((C_HEADS, 1, QB), F32),
                        pltpu.VMEM((C_HEADS, C_V, QB), F32)],
        compiler_params=_cparams("parallel", "arbitrary"),
        name="mla_prompt",
    )(qc, kc.reshape(bsz, t, CW), vc.reshape(bsz, t, 256), cm)


def _pool_body(u_ref, prev_ref, cnt_ref, pw_ref, ps_ref, y_ref, ns_ref, buf, *, sb, t):
    u = u_ref[...]
    buf[:, 0:1, :] = jnp.zeros((sb, 1, D_POOL), F32)
    buf[:, 1:16, :] = prev_ref[...]
    buf[:, 16:16 + t, :] = u
    sums = {}
    run = u
    for k in range(1, 16):
        run = run + buf[:, 16 - k:16 - k + t, :]
        if k + 1 in POOL_WINDOWS:
            sums[k + 1] = run
    lane = lax.broadcasted_iota(jnp.int32, (sb, t, D_POOL), 2)
    tot = jnp.where(lane < 64, sums[2],
                    jnp.where(lane < 128, sums[4], jnp.where(lane < 192, sums[8], sums[16])))
    d = (tot / cnt_ref[...][None] - u).astype(BF16)
    y = _dot(d.reshape(sb * t, D_POOL), pw_ref[...]) * ps_ref[...]
    y_ref[...] = y.astype(BF16)
    ns_ref[...] = buf[:, t + 1:t + 16, :]


def _pool(zs, prev, cnt, pw, ps, sb, t):
    n = zs.shape[0]
    nseq = n // t
    return pl.pallas_call(
        functools.partial(_pool_body, sb=sb, t=t),
        grid=(nseq // sb,),
        in_specs=[pl.BlockSpec((sb, t, D_POOL), lambda i: (i, 0, O_DU // D_POOL)),
                  pl.BlockSpec((sb, POOL_STATE, D_POOL), lambda i: (i, 0, 0)),
                  _resident(cnt.shape), _resident(pw.shape), _resident(ps.shape)],
        out_specs=[pl.BlockSpec((sb * t, D_POOL), lambda i: (i, 0)),
                   pl.BlockSpec((sb, POOL_STATE, D_POOL), lambda i: (i, 0, 0))],
        out_shape=[jax.ShapeDtypeStruct((n, D_POOL), BF16),
                   jax.ShapeDtypeStruct((nseq, POOL_STATE, D_POOL), F32)],
        scratch_shapes=[pltpu.VMEM((sb, t + 16, D_POOL), F32)],
        compiler_params=_cparams("parallel"),
        name="pool",
    )(zs.reshape(nseq, t, ZS), prev, cnt, pw, ps)


def _merge_body(x_ref, a_ref, b_ref, c_ref, d_ref, gl_ref, wb_ref, wo_ref, o_ref):
    merged = None
    for n, r in enumerate((a_ref, b_ref, c_ref, d_ref)):
        gate = jax.nn.sigmoid(gl_ref[:, n * D_MODEL:(n + 1) * D_MODEL])
        term = gate * _dot(r[...], wb_ref[n])
        merged = term if merged is None else merged + term
    o_ref[...] = x_ref[...] + _dot(merged.astype(BF16), wo_ref[...])


def _merge(x, oa, ob, oc, od, gl, wb, wo, tm):
    n = x.shape[0]
    br = pl.BlockSpec((tm, 256), lambda i: (i, 0))
    return pl.pallas_call(
        _merge_body,
        grid=(n // tm,),
        in_specs=[pl.BlockSpec((tm, D_MODEL), lambda i: (i, 0)), br, br, br, br,
                  pl.BlockSpec((tm, 4 * D_MODEL), lambda i: (i, 0)),
                  _resident(wb.shape), _resident(wo.shape)],
        out_specs=pl.BlockSpec((tm, D_MODEL), lambda i: (i, 0)),
        out_shape=jax.ShapeDtypeStruct((n, D_MODEL), F32),
        compiler_params=_cparams("parallel"),
        name="merge",
    )(x, oa, ob, oc, od, gl, wb, wo)


FC = 256
NF = D_FF // FC


def _ffn_body(h_ref, st_ref, g_ref, wup_ref, wdn_ref, cw_ref, cb_ref, o_ref, ns_ref,
              buf, carry, *, sb, t, tiles_per_seq):
    i = pl.program_id(0)
    first = (i % tiles_per_seq) == 0
    h = h_ref[...]
    hn = (h * lax.rsqrt(jnp.mean(h * h, axis=-1, keepdims=True) + EPS) * g_ref[...]).astype(BF16)
    acc = jnp.zeros((sb * t, D_MODEL), F32)
    for f in range(NF):
        sl = slice(f * FC, (f + 1) * FC)
        ga = _dot(hn, wup_ref[:, sl]).reshape(sb, t, FC)
        va = _dot(hn, wup_ref[:, D_FF + f * FC:D_FF + (f + 1) * FC]).reshape(sb, t, FC)
        @pl.when(first)
        def _(sl=sl):
            buf[:, 6:8, :] = st_ref[:, :, sl]

        @pl.when(jnp.logical_not(first))
        def _(f=f):
            buf[:, 6:8, :] = carry[f]

        buf[:, 8:8 + t, :] = ga
        conv = (cb_ref[:, sl][None] + buf[:, 6:6 + t, :] * cw_ref[0:1, sl][None]
                + buf[:, 7:7 + t, :] * cw_ref[1:2, sl][None] + ga * cw_ref[2:3, sl][None])
        act = (conv * jax.nn.sigmoid(conv) * va).reshape(sb * t, FC).astype(BF16)
        acc = acc + _dot(act, wdn_ref[sl, :])
        last = buf[:, 6 + t:8 + t, :]
        carry[f] = last
        ns_ref[:, :, sl] = last
    o_ref[...] = h + acc


def _ffn(h, st, g, wup, wdn, cw, cb, sb, t, tiles_per_seq):
    n = h.shape[0]
    tm = sb * t
    nst = st.shape[0]
    return pl.pallas_call(
        functools.partial(_ffn_body, sb=sb, t=t, tiles_per_seq=tiles_per_seq),
        grid=(n // tm,),
        in_specs=[pl.BlockSpec((tm, D_MODEL), lambda i: (i, 0)),
                  pl.BlockSpec((sb, 2, D_FF), lambda i: (i // tiles_per_seq, 0, 0)),
                  _resident((1, D_MODEL)), _resident(wup.shape), _resident(wdn.shape),
                  _resident(cw.shape), _resident(cb.shape)],
        out_specs=[pl.BlockSpec((tm, D_MODEL), lambda i: (i, 0)),
                   pl.BlockSpec((sb, 2, D_FF), lambda i: (i // tiles_per_seq, 0, 0))],
        out_shape=[jax.ShapeDtypeStruct((n, D_MODEL), F32),
                   jax.ShapeDtypeStruct((nst, 2, D_FF), F32)],
        scratch_shapes=[pltpu.VMEM((sb, t + 8, FC), F32), pltpu.VMEM((NF, sb, 2, FC), F32)],
        compiler_params=_cparams("arbitrary"),
        name="ffn",
    )(h, st, g, wup, wdn, cw, cb)


def _page_specs(cache, layer, gs, pp):
    tail = cache.shape[2:]
    zeros = (0,) * len(tail)
    return [pl.BlockSpec((None, None) + tail,
                         lambda b, g, pt, s=s, k=k: (layer, pt[b * gs + s, g * pp + k]) + zeros)
            for s in range(gs) for k in range(pp)]


def _sres(shape):
    nd = len(shape)
    return pl.BlockSpec(shape, lambda b, g, pt: (0,) * nd, pipeline_mode=pl.Buffered(1))


def _pad_rows(x, n):
    return jnp.concatenate([x, jnp.zeros((n - x.shape[0], x.shape[1]), x.dtype)], axis=0)


def _moba_sample_body(pt_ref, q_ref, new_ref, place_ref, als_ref, ald_ref, sc_ref, *rest,
                      gs, pp, ts, past_len, nblk):
    pages = rest[:gs * pp]
    o_ref = rest[gs * pp]
    qbd, m_ref, l_ref, acc_ref, ks_ref = rest[gs * pp + 1:]
    g = pl.program_id(1)
    rows = A_HEADS * ts
    hpk = A_HEADS // A_KV
    ppb = MOBA_BLOCK // PAGE
    bps = pp // ppb
    vk = A_KV * PAGE
    lane = lax.broadcasted_iota(jnp.int32, (rows, 128), 1)

    @pl.when(g == 0)
    def _():
        for s in range(gs):
            q = q_ref[s * ts:(s + 1) * ts, :]
            for h in range(A_HEADS):
                qbd[s, h * ts:(h + 1) * ts, :] = _dot(q, place_ref[h]).astype(BF16)
        m_ref[...] = jnp.zeros(m_ref.shape, F32)
        l_ref[...] = jnp.zeros(l_ref.shape, F32)

    units = [(s, nl) for s in range(gs) for nl in range(bps)]
    r16s, scs = {}, {}
    for s, nl in units:
        r16s[s, nl] = [pages[s * pp + nl * ppb + e][...].astype(BF16) for e in range(ppb)]
        ss = []
        for e in range(ppb):
            far = (past_len - (g * pp + nl * ppb + e) * PAGE).astype(F32)
            ss.append(_dot_nt(qbd[s], r16s[s, nl][e]) + als_ref[...] - sc_ref[...] * far)
        scs[s, nl] = jnp.concatenate(ss, axis=-1)
    stats = {}
    for u in units:
        m = jnp.max(scs[u], axis=-1, keepdims=True)
        p = jnp.exp(scs[u] - m).astype(BF16)
        stats[u] = (m, jnp.sum(p.astype(F32), axis=-1, keepdims=True), p)
    accs = {}
    for u in units:
        p = stats[u][2]
        acc = _dot(p[:, 0:vk], r16s[u][0])
        for e in range(1, ppb):
            acc = acc + _dot(p[:, e * vk:(e + 1) * vk], r16s[u][e])
        accs[u] = acc
    for s in range(gs):
        mm = m_ref[s]
        ll = l_ref[s]
        ksums = []
        for nl in range(bps):
            blk = g * bps + nl
            mm = jnp.where(lane == blk, stats[s, nl][0], mm)
            ll = jnp.where(lane == blk, stats[s, nl][1], ll)
            ksum = jnp.sum(pages[s * pp + nl * ppb][...].reshape(vk // 8, 8, 128), axis=0)
            for e in range(1, ppb):
                ksum = ksum + jnp.sum(pages[s * pp + nl * ppb + e][...].reshape(vk // 8, 8, 128), axis=0)
            ksums.append(ksum)
        acc_ref[s, pl.ds(g * bps, bps)] = jnp.stack([accs[s, nl] for nl in range(bps)])
        ks_ref[s, pl.ds(g * bps, bps)] = jnp.stack(ksums)
        m_ref[s] = mm
        l_ref[s] = ll

    @pl.when(g == pl.num_programs(1) - 1)
    def _():
        sub = lax.broadcasted_iota(jnp.int32, (nblk, 8, 128), 1)
        row_kv = lax.broadcasted_iota(jnp.int32, (rows, 128), 0) // (ts * hpk)
        bss, new16s, s_owns = [], [], []
        for s in range(gs):
            ks = ks_ref[s]
            bs = jnp.zeros((rows, 128), F32)
            for j in range(A_KV):
                kmean = jnp.sum(jnp.where(sub % A_KV == j, ks, 0.0), axis=1) * (1.0 / MOBA_BLOCK)
                hi, lo = _split16(_pad_rows(kmean, 128))
                bs = jnp.where(row_kv == j, _dot_nt(qbd[s], hi) + _dot_nt(qbd[s], lo), bs)
            bss.append(bs)
            new16s.append(_pad_rows(new_ref[s * ts * A_KV:(s + 1) * ts * A_KV, :], PAGE).astype(BF16))
            s_owns.append(_dot_nt(qbd[s], new16s[s]) + ald_ref[...])
        parts = []
        for s in range(gs):
            bs = bss[s]
            rank = jnp.zeros((rows, 128), F32)
            for mblk in range(nblk):
                col = bs[:, mblk:mblk + 1]
                rank = rank + jnp.where(col > bs, 1.0, jnp.where((col == bs) & (lane > mblk), 1.0, 0.0))
            sel = (lane < nblk) & (rank < MOBA_TOPK)
            m_own = jnp.max(s_owns[s], axis=-1, keepdims=True)
            p_own = jnp.exp(s_owns[s] - m_own).astype(BF16)
            mf = jnp.maximum(m_own, jnp.max(jnp.where(sel, m_ref[s], NEG), axis=-1, keepdims=True))
            wgt = jnp.where(sel, jnp.exp(m_ref[s] - mf), 0.0)
            w_own = jnp.exp(m_own - mf)
            lsum = (w_own * jnp.sum(p_own.astype(F32), axis=-1, keepdims=True)
                    + jnp.sum(wgt * l_ref[s], axis=-1, keepdims=True))
            parts.append((p_own, w_own, wgt, lsum))
        pv_owns = [_dot(parts[s][0], new16s[s]) for s in range(gs)]
        for s in range(gs):
            _, w_own, wgt, lsum = parts[s]
            acc = w_own * pv_owns[s]
            for n in range(nblk):
                acc = acc + wgt[:, n:n + 1] * acc_ref[s, n]
            o = acc / lsum
            for h in range(A_HEADS):
                o_ref[s * ts:(s + 1) * ts, h * 64:(h + 1) * 64] = (
                    o[h * ts:(h + 1) * ts, 64:128].astype(BF16))


def _sample_grid(bs, n_pages, gs, pp, q_spec, new_spec, consts_specs, cache, layer, scratch):
    return pltpu.PrefetchScalarGridSpec(
        num_scalar_prefetch=1,
        grid=(bs // gs, n_pages // pp),
        in_specs=[pl.BlockSpec(q_spec, lambda b, g, pt: (b, 0)),
                  pl.BlockSpec(new_spec, lambda b, g, pt: (b, 0))]
        + consts_specs + _page_specs(cache, layer, gs, pp),
        out_specs=pl.BlockSpec((q_spec[0], 256), lambda b, g, pt: (b, 0)),
        scratch_shapes=scratch,
    )


def _moba_sample(qa, a_new, cache, layer, page_table, consts, bs, ts, gs, pp):
    n_pages = page_table.shape[1]
    past_len = n_pages * PAGE
    nblk = past_len // MOBA_BLOCK
    assert nblk <= 128 and A_KV * ts <= PAGE
    rows = A_HEADS * ts
    cs = [consts[k] for k in ("sa_place", "sa_als", "sa_ald", "sa_sc")]
    cache2 = cache.reshape(cache.shape[0], cache.shape[1], A_KV * PAGE, 128)
    grid_spec = _sample_grid(
        bs, n_pages, gs, pp, (gs * ts, 256), (gs * ts * A_KV, 128), [_sres(c.shape) for c in cs],
        cache2, layer,
        [pltpu.VMEM((gs, rows, 128), BF16), pltpu.VMEM((gs, rows, 128), F32),
         pltpu.VMEM((gs, rows, 128), F32), pltpu.VMEM((gs, nblk, rows, 128), F32),
         pltpu.VMEM((gs, nblk, 8, 128), F32)])
    return pl.pallas_call(
        functools.partial(_moba_sample_body, gs=gs, pp=pp, ts=ts, past_len=past_len, nblk=nblk),
        grid_spec=grid_spec,
        out_shape=jax.ShapeDtypeStruct((bs * ts, 256), BF16),
        compiler_params=_cparams("parallel", "arbitrary"),
        name="moba_sample",
    )(page_table, qa, a_new.reshape(bs * ts * A_KV, 128), *cs, *([cache2] * (gs * pp)))


def _softmax_step(s, pv_fn, m_ref, l_ref, acc_ref, idx):
    m_prev = m_ref[idx]
    m_new = jnp.maximum(m_prev, jnp.max(s, axis=-1, keepdims=True))
    alpha = jnp.exp(m_prev - m_new)
    p = jnp.exp(s - m_new).astype(BF16)
    m_ref[idx] = m_new
    l_ref[idx] = alpha * l_ref[idx] + jnp.sum(p.astype(F32), axis=-1, keepdims=True)
    acc_ref[idx] = alpha * acc_ref[idx] + pv_fn(p)


def _softmax_multi(scores, pv_fns, m_ref, l_ref, acc_ref):
    parts = []
    for idx, s in enumerate(scores):
        m_prev = m_ref[idx]
        m_new = jnp.maximum(m_prev, jnp.max(s, axis=-1, keepdims=True))
        p = jnp.exp(s - m_new).astype(BF16)
        parts.append((m_new, jnp.exp(m_prev - m_new), jnp.sum(p.astype(F32), axis=-1, keepdims=True), p))
    pvs = [fn(part[3]) for fn, part in zip(pv_fns, parts)]
    for idx, ((m_new, alpha, psum, _), pv) in enumerate(zip(parts, pvs)):
        m_ref[idx] = m_new
        l_ref[idx] = alpha * l_ref[idx] + psum
        acc_ref[idx] = alpha * acc_ref[idx] + pv


def _pv_pages(p, r16):
    pv = _dot(p[:, 0:PAGE], r16[0])
    for k in range(1, len(r16)):
        pv = pv + _dot(p[:, k * PAGE:(k + 1) * PAGE], r16[k])
    return pv


def _diff_sample_body(pt_ref, q_ref, new_ref, place_ref, lam_ref, gs_ref, als_ref, ald_ref, sc_ref,
                      *rest, gs, pp, ts, past_len, lam_init):
    pages = rest[:gs * pp]
    o_ref = rest[gs * pp]
    qbd, m_ref, l_ref, acc_ref = rest[gs * pp + 1:]
    g = pl.program_id(1)
    half = B_HEADS * ts

    @pl.when(g == 0)
    def _():
        for s in range(gs):
            q = q_ref[s * ts:(s + 1) * ts, :]
            for e in range(2 * B_HEADS):
                qbd[s, e * ts:(e + 1) * ts, :] = _dot(q, place_ref[e]).astype(BF16)
        m_ref[...] = jnp.full(m_ref.shape, NEG, F32)
        l_ref[...] = jnp.zeros(l_ref.shape, F32)
        acc_ref[...] = jnp.zeros(acc_ref.shape, F32)

    r16s = [[pages[s * pp + k][...].astype(BF16) for k in range(pp)] for s in range(gs)]
    scores = []
    for s in range(gs):
        ss = []
        for k in range(pp):
            far = (past_len - (g * pp + k) * PAGE).astype(F32)
            ss.append(_dot_nt(qbd[s], r16s[s][k]) + als_ref[...] - sc_ref[...] * far)
        scores.append(jnp.concatenate(ss, axis=-1))
    _softmax_multi(scores, [functools.partial(_pv_pages, r16=r16s[s]) for s in range(gs)],
                   m_ref, l_ref, acc_ref)

    @pl.when(g == pl.num_programs(1) - 1)
    def _():
        lam = _lam_of(lam_ref[...], lam_init)
        for s in range(gs):
            new16 = _pad_rows(new_ref[s * ts:(s + 1) * ts, :], PAGE).astype(BF16)
            s_own = _dot_nt(qbd[s], new16) + ald_ref[...]
            _softmax_step(s_own, lambda p, new16=new16: _dot(p, new16), m_ref, l_ref, acc_ref, s)
            o = acc_ref[s] / l_ref[s]
            o = o[0:half, 64:128] - lam * o[half:2 * half, 64:128]
            o = (o * lax.rsqrt(jnp.mean(o * o, axis=-1, keepdims=True) + EPS) * gs_ref[...]
                 * (1.0 - lam_init))
            for h in range(B_HEADS):
                o_ref[s * ts:(s + 1) * ts, h * 64:(h + 1) * 64] = o[h * ts:(h + 1) * ts].astype(BF16)


def _diff_sample(qb, b_new, b_lam, g_sub, cache, layer, page_table, consts, bs, ts, gs, pp, lam_init):
    n_pages = page_table.shape[1]
    past_len = n_pages * PAGE
    rows = 2 * B_HEADS * ts
    cs = [consts["sb_place"], b_lam, g_sub, consts["sb_als"], consts["sb_ald"], consts["sb_sc"]]
    cache2 = cache.reshape(cache.shape[0], cache.shape[1], PAGE, 128)
    grid_spec = _sample_grid(
        bs, n_pages, gs, pp, (gs * ts, 256), (gs * ts, 128), [_sres(c.shape) for c in cs], cache2, layer,
        [pltpu.VMEM((gs, rows, 128), BF16), pltpu.VMEM((gs, rows, 1), F32),
         pltpu.VMEM((gs, rows, 1), F32), pltpu.VMEM((gs, rows, 128), F32)])
    return pl.pallas_call(
        functools.partial(_diff_sample_body, gs=gs, pp=pp, ts=ts, past_len=past_len,
                          lam_init=lam_init),
        grid_spec=grid_spec,
        out_shape=jax.ShapeDtypeStruct((bs * ts, 256), BF16),
        compiler_params=_cparams("parallel", "arbitrary"),
        name="diff_sample",
    )(page_table, qb, b_new, *cs, *([cache2] * (gs * pp)))


def _mla_sample_body(pt_ref, q_ref, new_ref, mq_ref, wext_ref, wuv_ref, cm_ref, *rest, gs, pp, ts):
    pages = rest[:gs * pp]
    o_ref = rest[gs * pp]
    lhs, pbuf, nbuf, m_ref, l_ref, acc_ref = rest[gs * pp + 1:]
    b = pl.program_id(0)
    g = pl.program_id(1)
    rows = C_HEADS * ts
    next_rows = C_HEADS * C_NOPE + C_ROPE
    width = C_KVL + C_ROPE

    @pl.when((b == 0) & (g == 0))
    def _():
        pbuf[...] = jnp.zeros(pbuf.shape, BF16)
        nbuf[...] = jnp.zeros(nbuf.shape, F32)
        for s in range(gs):
            lhs[s, rows:rows + next_rows, :] = wext_ref[...]

    @pl.when(g == 0)
    def _():
        for s in range(gs):
            q = q_ref[s * ts:(s + 1) * ts, :]
            for h in range(C_HEADS):
                lhs[s, h * ts:(h + 1) * ts, :] = _dot(q, mq_ref[h]).astype(BF16)
        m_ref[...] = jnp.full(m_ref.shape, NEG, F32)
        l_ref[...] = jnp.zeros(l_ref.shape, F32)
        acc_ref[...] = jnp.zeros(acc_ref.shape, F32)

    def scores(s, kt16):
        out = _dot(lhs[s], kt16)
        kt = out[rows:rows + next_rows]
        kt2 = kt * kt
        ssq = jnp.sum(kt2[0:C_HEADS * C_NOPE].reshape(C_HEADS, C_NOPE, PAGE), axis=1)
        ssq = ssq + jnp.sum(kt2[C_HEADS * C_NOPE:], axis=0, keepdims=True)
        r = lax.rsqrt(ssq * (1.0 / C_QK) + EPS)
        r_rows = jnp.broadcast_to(r[:, None, :], (C_HEADS, ts, PAGE)).reshape(rows, PAGE)
        return out[0:rows] * r_rows

    def pv_pages(p, s):
        pv = _dot_nt(p[:, 0:PAGE], pbuf[s, 0, 0:C_KVL, :])
        for k in range(1, pp):
            pv = pv + _dot_nt(p[:, k * PAGE:(k + 1) * PAGE], pbuf[s, k, 0:C_KVL, :])
        return pv

    all_scores = []
    for s in range(gs):
        ss = []
        for k in range(pp):
            pbuf[s, k, 0:width, :] = pages[s * pp + k][...].astype(BF16)
            ss.append(scores(s, pbuf[s, k]))
        all_scores.append(jnp.concatenate(ss, axis=-1))
    _softmax_multi(all_scores, [functools.partial(pv_pages, s=s) for s in range(gs)],
                   m_ref, l_ref, acc_ref)

    @pl.when(g == pl.num_programs(1) - 1)
    def _():
        for s in range(gs):
            nbuf[0:ts, 0:width] = new_ref[s * ts:(s + 1) * ts, :]
            nkt = nbuf[...].T.astype(BF16)
            _softmax_step(scores(s, nkt) + cm_ref[...],
                          lambda p, nkt=nkt: _dot_nt(p, nkt[0:C_KVL, :]), m_ref, l_ref, acc_ref, s)
            o_lat = (acc_ref[s] / l_ref[s]).astype(BF16)
            for h in range(C_HEADS):
                o_ref[s * ts:(s + 1) * ts, h * C_V:(h + 1) * C_V] = _dot(
                    o_lat[h * ts:(h + 1) * ts], wuv_ref[:, h * C_V:(h + 1) * C_V]).astype(BF16)


def _mla_sample(qce, c_new, mq, wext, wuv, cache, layer, page_table, consts, bs, ts, gs, pp):
    n_pages = page_table.shape[1]
    rows = C_HEADS * ts
    cs = [mq, wext, wuv, consts["sc_cm"]]
    cache_t = jnp.swapaxes(cache, 2, 3)
    grid_spec = _sample_grid(
        bs, n_pages, gs, pp, (gs * ts, CW), (gs * ts, 160), [_sres(c.shape) for c in cs], cache_t, layer,
        [pltpu.VMEM((gs, rows + wext.shape[0], 256), BF16),
         pltpu.VMEM((gs, pp, 256, PAGE), BF16), pltpu.VMEM((PAGE, 256), F32),
         pltpu.VMEM((gs, rows, 1), F32), pltpu.VMEM((gs, rows, 1), F32),
         pltpu.VMEM((gs, rows, C_KVL), F32)])
    return pl.pallas_call(
        functools.partial(_mla_sample_body, gs=gs, pp=pp, ts=ts),
        grid_spec=grid_spec,
        out_shape=jax.ShapeDtypeStruct((bs * ts, 256), BF16),
        compiler_params=_cparams("arbitrary", "arbitrary"),
        name="mla_sample",
    )(page_table, qce, c_new, *cs, *([cache_t] * (gs * pp)))


def _alibi(n):
    return np.asarray(2.0 ** (-8.0 * np.arange(1, n + 1) / n), dtype=np.float32)


def _group_matrix(width, group):
    idx = np.arange(width) // group
    return jnp.asarray(idx[:, None] == idx[None, :], dtype=BF16)


def _rope_tables(pos, reps):
    half = C_ROPE // 2
    inv = ROPE_THETA ** (-jnp.arange(half, dtype=F32) / half)
    ang = pos.astype(F32)[:, None] * inv[None, :]
    cos, sin = jnp.cos(ang), jnp.sin(ang)
    n = pos.shape[0]
    one, zero = jnp.ones((n, C_NOPE), F32), jnp.zeros((n, C_NOPE), F32)
    zh = jnp.zeros((n, half), F32)
    zp = jnp.zeros((n, CP - C_QK), F32)
    qcos = jnp.concatenate([one, cos, cos, zp] * C_HEADS, axis=-1)
    qsa = jnp.concatenate([zero, -sin, zh, zp] * C_HEADS, axis=-1)
    qsb = jnp.concatenate([zero, zh, sin, zp] * C_HEADS, axis=-1)
    pad = jnp.zeros((n, 128 - C_ROPE), F32)
    kcos = jnp.concatenate([cos, cos, pad], axis=-1)
    ksa = jnp.concatenate([-sin, zh, pad], axis=-1)
    ksb = jnp.concatenate([zh, sin, pad], axis=-1)
    rt = jnp.concatenate([qcos, qsa, qsb, kcos, ksa, ksb], axis=-1)
    return jnp.tile(rt, (reps, 1))


def _pool_counts(pos):
    win = np.repeat(np.asarray(POOL_WINDOWS), POOL_GROUP)[None, :]
    return jnp.asarray(np.minimum(win, np.asarray(pos)[:, None] + 1), dtype=F32)


def _make_consts(tp, ts, past_len, tm_s):
    c = {}
    sl = _alibi(A_HEADS)
    i = np.arange(QB)[:, None]
    r = np.arange(QB)[None, :]
    dist = (i - r).astype(np.float32)
    causal = i >= r

    def alibi_tables(heads):
        al0 = np.concatenate([-sl[h] * dist.T for h in heads], axis=1)
        ald = np.concatenate([np.where(causal.T, -sl[h] * dist.T, NEG) for h in heads], axis=1)
        sc = np.concatenate([np.full((1, QB), sl[h], np.float32) for h in heads], axis=1)
        return ald.astype(np.float32), al0.astype(np.float32), sc

    hpk = A_HEADS // A_KV
    per = [alibi_tables(range(j * hpk, (j + 1) * hpk)) for j in range(A_KV)]
    c["a_ald"], c["a_al0"], c["a_sc"] = (jnp.asarray(np.stack([p[k] for p in per])) for k in range(3))
    c["b_ald"], c["b_al0"], c["b_sc"] = (jnp.asarray(x) for x in alibi_tables(range(B_HEADS)))
    c["c_cm"] = jnp.asarray(np.where(causal.T, 0.0, NEG).astype(np.float32))
    nb = tp // MOBA_BLOCK
    assert nb + MOBA_TOPK <= 32 and tp % QB == 0
    t_idx = np.arange(tp)[:, None] // MOBA_BLOCK
    lane = np.arange(128)[None, :]
    c["a_avg"] = jnp.asarray(np.where(t_idx == lane % 32, 1.0 / MOBA_BLOCK, 0.0), dtype=BF16)
    c["rt_p"] = _rope_tables(jnp.arange(tp, dtype=jnp.int32), 1)
    c["rt_s"] = _rope_tables(past_len + jnp.arange(ts, dtype=jnp.int32), tm_s // ts)
    c["cnt_p"] = _pool_counts(np.arange(tp))
    c["cnt_s"] = _pool_counts(past_len + np.arange(ts))
    c["g64"], c["g32"], c["g96"] = _group_matrix(256, 64), _group_matrix(256, 32), _group_matrix(CW, CP)

    assert past_len % MOBA_BLOCK == 0 and ts <= PAGE and MOBA_BLOCK % PAGE == 0
    qi = np.tile(np.arange(ts), A_HEADS)[:, None]
    kr = np.arange(PAGE)[None, :]
    slr = np.repeat(sl, ts)[:, None]
    als = (-slr * (qi - kr)).astype(np.float32)
    own_ok = (kr <= qi) & (kr < ts)
    ald = np.where(own_ok, als, NEG).astype(np.float32)
    c["sb_als"], c["sb_ald"], c["sb_sc"] = (jnp.asarray(np.concatenate([x, x], axis=0))
                                            for x in (als, ald, slr))
    c["sc_cm"] = jnp.asarray(np.where(own_ok, 0.0, NEG).astype(np.float32))
    vr = np.arange(A_KV * PAGE)[None, :]
    row_kv = (np.repeat(np.arange(A_HEADS), ts) // hpk)[:, None]
    mine = (vr % A_KV) == row_kv
    als_a = np.where(mine, -slr * (qi - vr // A_KV), NEG).astype(np.float32)
    vo = np.arange(PAGE)[None, :]
    own_a = ((vo % A_KV) == row_kv) & (vo // A_KV <= qi) & (vo // A_KV < ts)
    ald_a = np.where(own_a, -slr * (qi - vo // A_KV), NEG).astype(np.float32)
    c["sa_als"], c["sa_ald"], c["sa_sc"] = jnp.asarray(als_a), jnp.asarray(ald_a), jnp.asarray(slr)
    pa = np.zeros((A_HEADS, 256, 128), np.float32)
    for h in range(A_HEADS):
        pa[h, h * 64 + np.arange(64), np.arange(64)] = 1.0
    c["sa_place"] = jnp.asarray(pa, dtype=BF16)
    pb = np.zeros((2 * B_HEADS, 256, 128), np.float32)
    for e in range(2):
        for h in range(B_HEADS):
            pb[e * B_HEADS + h, h * 64 + e * 32 + np.arange(32), e * 32 + np.arange(32)] = 1.0
    c["sb_place"] = jnp.asarray(pb, dtype=BF16)
    return c


def _layer_weights(l, p):
    w = {}
    w_in = p["w_in"][l]
    sizes = (256, 128, 128, 256, 64, 64, 256, 160, 256, 4096)
    offs = np.concatenate([[0], np.cumsum(sizes)])
    aq, ak, av, bq, bk, bv, cq, ckv, du, gl = (w_in[:, offs[k]:offs[k + 1]] for k in range(10))
    akv = jnp.concatenate([ak[:, 0:64], av[:, 0:64], ak[:, 64:128], av[:, 64:128]], axis=1)
    pad = jnp.zeros((D_MODEL, 96), F32)
    w["ws"] = jnp.concatenate([aq, akv, bq, cq, du, ckv, pad, bk, bv], axis=1).astype(BF16)
    w["wg"] = gl.astype(BF16)
    w["g1"] = p["norm1_g"][l][None]
    w["g2"] = p["norm2_g"][l][None]

    def row(v):
        return jnp.pad(v, (0, CW - v.shape[0]))[None]

    def heads_padded(v):
        return jnp.tile(jnp.pad(v, (0, CP - C_QK)), C_HEADS)

    one64 = jnp.ones((64,), F32)
    zero64 = jnp.zeros((64,), F32)
    gains = [
        row(jnp.tile(p["a_q_g"][l], 4) * (A_DIM ** -0.5)),
        row(jnp.concatenate([p["a_k_g"][l], one64] * 2)),
        row(jnp.concatenate([one64, zero64] * 2)),
        row(jnp.tile(p["b_q_g"][l], 8) * (B_QK ** -0.5)),
        row(jnp.concatenate([p["b_k_g"][l], p["b_k_g"][l], one64])),
        row(jnp.concatenate([one64, zero64])),
        row(p["c_qa_g"][l]),
        row(heads_padded(p["c_q_g"][l]) * (C_QK ** -0.5)),
        row(p["c_kva_g"][l]),
        row(heads_padded(p["c_k_g"][l])),
    ]
    w["gains"] = jnp.concatenate(gains + [jnp.zeros((16 - len(gains), CW), F32)], axis=0)
    w["wuq"] = jnp.pad(p["w_uq"][l], ((0, 0), (0, 0), (0, CP - C_QK))).reshape(C_QL, CW).astype(BF16)
    w_uk = p["w_uk"][l]
    eye = jnp.eye(C_ROPE, dtype=F32)
    top = jnp.concatenate([jnp.concatenate([w_uk[:, h], jnp.zeros((C_KVL, CP - C_NOPE), F32)], axis=1)
                           for h in range(C_HEADS)], axis=1)
    mid = jnp.concatenate([jnp.concatenate([jnp.zeros((C_ROPE, C_NOPE), F32), eye,
                                            jnp.zeros((C_ROPE, CP - C_QK), F32)], axis=1)
                           for _ in range(C_HEADS)], axis=1)
    w["wkc"] = jnp.concatenate([top, mid, jnp.zeros((96, CW), F32)], axis=0).astype(BF16)
    w["wuv"] = p["w_uv"][l].reshape(C_KVL, C_HEADS * C_V).astype(BF16)
    mq = []
    for h in range(C_HEADS):
        m = jnp.zeros((CW, 256), F32)
        m = m.at[h * CP:h * CP + C_NOPE, 0:C_KVL].set(w_uk[:, h].T)
        m = m.at[h * CP + C_NOPE:h * CP + C_QK, C_KVL:C_KVL + C_ROPE].set(eye)
        mq.append(m)
    w["mq"] = jnp.stack(mq).astype(BF16)
    probe = jnp.concatenate([w_uk[:, h].T for h in range(C_HEADS)], axis=0)
    probe = jnp.concatenate([probe, jnp.zeros((C_HEADS * C_NOPE, 128), F32)], axis=1)
    rope_rows = jnp.zeros((C_ROPE, 256), F32).at[:, C_KVL:C_KVL + C_ROPE].set(eye)
    w["wext"] = jnp.concatenate([probe, rope_rows], axis=0).astype(BF16)
    pw = jnp.zeros((D_POOL, D_POOL), F32)
    for gi in range(len(POOL_WINDOWS)):
        sl = slice(gi * POOL_GROUP, (gi + 1) * POOL_GROUP)
        pw = pw.at[sl, sl].set(p["pool_w"][l][gi])
    w["pw"] = pw.astype(BF16)
    w["ps"] = p["pool_scale"][l][None]
    w["wb"] = p["w_branch"][l].astype(BF16)
    w["wo"] = p["w_out"][l].astype(BF16)
    w["wup"] = p["w_up"][l].astype(BF16)
    w["wdn"] = p["w_down"][l].astype(BF16)
    w["cw"] = p["conv_w"][l]
    w["cb"] = p["conv_b"][l][None]
    w["b_lam"] = p["b_lam"][l]
    w["g_sub"] = p["b_sub_g"][l][None]
    return w


def _tile(n, pref):
    return pref if n % pref == 0 else n


def kernel(x_prompt, x_sample, cache_a, cache_b, cache_c, state_pool, state_conv, page_table,
           norm1_g, w_in, a_q_g, a_k_g, b_q_g, b_k_g, b_lam, b_sub_g, c_qa_g, c_kva_g,
           w_uq, w_uk, w_uv, c_q_g, c_k_g, pool_w, pool_scale, w_branch, w_out,
           norm2_g, w_up, conv_w, conv_b, w_down):
    params = dict(norm1_g=norm1_g, w_in=w_in, a_q_g=a_q_g, a_k_g=a_k_g, b_q_g=b_q_g, b_k_g=b_k_g,
                  b_lam=b_lam, b_sub_g=b_sub_g, c_qa_g=c_qa_g, c_kva_g=c_kva_g, w_uq=w_uq,
                  w_uk=w_uk, w_uv=w_uv, c_q_g=c_q_g, c_k_g=c_k_g, pool_w=pool_w,
                  pool_scale=pool_scale, w_branch=w_branch, w_out=w_out, norm2_g=norm2_g,
                  w_up=w_up, conv_w=conv_w, conv_b=conv_b, w_down=w_down)
    bp, tp, _ = x_prompt.shape
    bs, ts, _ = x_sample.shape
    depth = w_in.shape[0]
    n_pages = page_table.shape[1]
    past_len = n_pages * PAGE
    n_s = bs * ts
    tm_p = _tile(tp, 256)
    tm_s = _tile(n_s, 256)
    tf_p = _tile(tp, 512)
    sb_s = tm_s // ts
    pp = 16 if n_pages % 16 == 0 else n_pages
    gs = 4 if bs % 4 == 0 else 1
    consts = _make_consts(tp, ts, past_len, tm_s)
    zero_pool = jnp.zeros((bp, POOL_STATE, D_POOL), F32)
    zero_conv = jnp.zeros((bp, 2, D_FF), F32)

    xp = x_prompt.reshape(bp * tp, D_MODEL)
    xs = x_sample.reshape(n_s, D_MODEL)
    new_p = [[] for _ in range(5)]
    new_s = [[] for _ in range(5)]
    for l in range(depth):
        w = _layer_weights(l, params)
        lam_init = 0.8 - 0.6 * math.exp(-0.3 * l)
        zs, gl = _inproj(xp, w["g1"], w["ws"], w["wg"], tm_p)
        qa, a_new, qb, b_new, qc, _, c_new, kc, vc = _prep(
            zs, consts["rt_p"], w["gains"], consts["g64"], consts["g32"], consts["g96"],
            w["wuq"], w["wkc"], w["wuv"], tm_p)
        oa = _moba_prompt(qa, a_new, consts, bp, tp)
        ob = _diff_prompt(qb, b_new, w["b_lam"], w["g_sub"], consts, bp, tp, lam_init)
        oc = _mla_prompt(qc, kc, vc, consts, bp, tp)
        od, pool_new = _pool(zs, zero_pool, consts["cnt_p"], w["pw"], w["ps"], 1, tp)
        h = _merge(xp, oa, ob, oc, od, gl, w["wb"], w["wo"], tm_p)
        xp, conv_new = _ffn(h, zero_conv, w["g2"], w["wup"], w["wdn"], w["cw"], w["cb"],
                            1, tf_p, tp // tf_p)
        for k, v in enumerate((a_new.reshape(bp, tp, A_KV, 128), b_new.reshape(bp, tp, 1, 128),
                               c_new.reshape(bp, tp, 160), pool_new, conv_new)):
            new_p[k].append(v)
        zs, gl = _inproj(xs, w["g1"], w["ws"], w["wg"], tm_s)
        qa, a_new, qb, b_new, _, qce, c_new, _, _ = _prep(
            zs, consts["rt_s"], w["gains"], consts["g64"], consts["g32"], consts["g96"],
            w["wuq"], w["wkc"], w["wuv"], tm_s)
        oa = _moba_sample(qa, a_new, cache_a, l, page_table, consts, bs, ts, gs, pp)
        ob = _diff_sample(qb, b_new, w["b_lam"], w["g_sub"], cache_b, l, page_table, consts, bs, ts,
                          gs, pp, lam_init)
        oc = _mla_sample(qce, c_new, w["mq"], w["wext"], w["wuv"], cache_c, l, page_table, consts,
                         bs, ts, gs, pp)
        od, pool_new = _pool(zs, state_pool[l], consts["cnt_s"], w["pw"], w["ps"], sb_s, ts)
        h = _merge(xs, oa, ob, oc, od, gl, w["wb"], w["wo"], tm_s)
        xs, conv_new = _ffn(h, state_conv[l], w["g2"], w["wup"], w["wdn"], w["cw"], w["cb"],
                            sb_s, ts, 1)
        for k, v in enumerate((a_new.reshape(bs, ts, A_KV, 128), b_new.reshape(bs, ts, 1, 128),
                               c_new.reshape(bs, ts, 160), pool_new, conv_new)):
            new_s[k].append(v)
    np_ = [jnp.stack(v, axis=0) for v in new_p]
    ns_ = [jnp.stack(v, axis=0) for v in new_s]
    return (xp.reshape(bp, tp, D_MODEL), xs.reshape(bs, ts, D_MODEL),
            np_[0], np_[1], np_[2], np_[3], np_[4], ns_[0], ns_[1], ns_[2], ns_[3], ns_[4])
```

```python
import functools
import math

import numpy as np
import jax
import jax.numpy as jnp
from jax import lax
from jax.experimental import pallas as pl
from jax.experimental.pallas import tpu as pltpu

F32 = jnp.float32
BF16 = jnp.bfloat16

D_MODEL = 1024
PAGE = 128
A_HEADS, A_KV, A_DIM = 4, 2, 64
MOBA_BLOCK, MOBA_TOPK = 256, 3
B_HEADS, B_QK, B_V = 4, 32, 64
C_HEADS, C_QL, C_KVL, C_NOPE, C_ROPE, C_V = 4, 256, 128, 64, 32, 64
C_QK = C_NOPE + C_ROPE
CP = 128
CW = C_HEADS * CP
ROPE_THETA = 10000.0
POOL_WINDOWS = (2, 4, 8, 16)
POOL_GROUP = 64
D_POOL = 256
POOL_STATE = 15
D_FF = 2816
EPS = 1e-6
NEG = -1e30

ZS = 1664
O_AQ, O_AKV, O_BQ, O_CQ, O_DU, O_CKV, O_BKV = 0, 256, 512, 768, 1024, 1280, 1536

QB = 256
VMEM_LIMIT = 56 * 1024 * 1024


def _cparams(*sem):
    return pltpu.CompilerParams(dimension_semantics=sem, vmem_limit_bytes=VMEM_LIMIT)


def _resident(shape):
    nd = len(shape)
    return pl.BlockSpec(shape, lambda *_: (0,) * nd, pipeline_mode=pl.Buffered(1))


def _dot(a, b):
    return jnp.dot(a, b, preferred_element_type=F32)


def _dot_nt(a, b):
    return lax.dot_general(a, b, (((1,), (1,)), ((), ())), preferred_element_type=F32)


def _split16(x):
    hi = x.astype(BF16)
    lo = (x - hi.astype(F32)).astype(BF16)
    return hi, lo


def _inproj_body(x_ref, g_ref, ws_ref, wg_ref, zs_ref, gl_ref):
    x = x_ref[...]
    ms = jnp.mean(x * x, axis=-1, keepdims=True)
    xn = (x * lax.rsqrt(ms + EPS) * g_ref[...]).astype(BF16)
    zs_ref[...] = _dot(xn, ws_ref[...])
    for c in range(4):
        sl = slice(c * D_MODEL, (c + 1) * D_MODEL)
        gl_ref[:, sl] = _dot(xn, wg_ref[:, sl])


def _inproj(x, g, ws, wg, tm):
    n = x.shape[0]
    return pl.pallas_call(
        _inproj_body,
        grid=(n // tm,),
        in_specs=[pl.BlockSpec((tm, D_MODEL), lambda i: (i, 0)),
                  _resident((1, D_MODEL)), _resident(ws.shape), _resident(wg.shape)],
        out_specs=[pl.BlockSpec((tm, ZS), lambda i: (i, 0)),
                   pl.BlockSpec((tm, 4 * D_MODEL), lambda i: (i, 0))],
        out_shape=[jax.ShapeDtypeStruct((n, ZS), F32),
                   jax.ShapeDtypeStruct((n, 4 * D_MODEL), F32)],
        compiler_params=_cparams("parallel"),
        name="inproj",
    )(x, g, ws, wg)


G_AQ, G_AK, M_AK, G_BQ, G_BK, M_BK, G_CQA, G_CQ, G_CKVA, G_CK = range(10)


def _gnorm(x, gmat, n):
    ss = _dot((x * x).astype(BF16), gmat)
    return x * lax.rsqrt(ss * (1.0 / n) + EPS)


def _rope_cols(x, cos, sina, sinb):
    outs = []
    for c in range(x.shape[1] // 128):
        sl = slice(c * 128, (c + 1) * 128)
        xc = x[:, sl]
        outs.append(xc * cos[:, sl] + pltpu.roll(xc, 112, 1) * sina[:, sl]
                    + pltpu.roll(xc, 16, 1) * sinb[:, sl])
    return outs[0] if len(outs) == 1 else jnp.concatenate(outs, axis=-1)


def _prep_body(zs_ref, rt_ref, gn_ref, g64_ref, g32_ref, g96_ref, wuq_ref, wkc_ref, wuv_ref,
               qa_ref, an_ref, qb_ref, bn_ref, qc_ref, qce_ref, cn_ref, kc_ref, vc_ref):
    def gain(row, w):
        return gn_ref[row:row + 1, :w]

    aq = zs_ref[:, O_AQ:O_AQ + 256]
    qa_ref[...] = (_gnorm(aq, g64_ref[...], A_DIM) * gain(G_AQ, 256)).astype(BF16)
    akv = zs_ref[:, O_AKV:O_AKV + 256]
    an_ref[...] = jnp.where(gain(M_AK, 256) > 0.5,
                            _gnorm(akv, g64_ref[...], A_DIM) * gain(G_AK, 256), akv)
    bq = zs_ref[:, O_BQ:O_BQ + 256]
    qb_ref[...] = (_gnorm(bq, g32_ref[...], B_QK) * gain(G_BQ, 256)).astype(BF16)
    bkv = zs_ref[:, O_BKV:O_BKV + 128]
    bn_ref[...] = jnp.where(gain(M_BK, 128) > 0.5,
                            _gnorm(bkv, g32_ref[:128, :128], B_QK) * gain(G_BK, 128), bkv)
    cq = zs_ref[:, O_CQ:O_CQ + 256]
    cqn = cq * lax.rsqrt(jnp.mean(cq * cq, axis=-1, keepdims=True) + EPS) * gain(G_CQA, 256)
    qc = _dot(cqn.astype(BF16), wuq_ref[...])
    qc = _rope_cols(qc, rt_ref[:, 0:CW], rt_ref[:, CW:2 * CW], rt_ref[:, 2 * CW:3 * CW])
    qcn = _gnorm(qc, g96_ref[...], C_QK) * gain(G_CQ, CW)
    qc_ref[...] = qcn.astype(BF16)
    qce_ref[...] = (qcn * gain(G_CK, CW)).astype(BF16)
    lat = zs_ref[:, O_CKV:O_CKV + 128]
    latn = lat * lax.rsqrt(jnp.mean(lat * lat, axis=-1, keepdims=True) + EPS) * gain(G_CKVA, 128)
    kr = _rope_cols(zs_ref[:, O_CKV + 128:O_CKV + 256], rt_ref[:, 3 * CW:3 * CW + 128],
                    rt_ref[:, 3 * CW + 128:3 * CW + 256], rt_ref[:, 3 * CW + 256:3 * CW + 384])
    cn_ref[:, 0:128] = latn
    cn_ref[:, 128:160] = kr[:, 0:32]
    cfull = jnp.concatenate([latn, kr], axis=-1).astype(BF16)
    kc = _gnorm(_dot(cfull, wkc_ref[...]), g96_ref[...], C_QK) * gain(G_CK, CW)
    kc_ref[...] = kc.astype(BF16)
    vc_ref[...] = _dot(latn.astype(BF16), wuv_ref[...]).astype(BF16)


def _prep(zs, rt, gains, g64, g32, g96, wuq, wkc, wuv, tm):
    n = zs.shape[0]
    nt = rt.shape[0] // tm
    outs = [(256, BF16), (256, F32), (256, BF16), (128, F32), (CW, BF16), (CW, BF16),
            (160, F32), (CW, BF16), (256, BF16)]
    return pl.pallas_call(
        _prep_body,
        grid=(n // tm,),
        in_specs=[pl.BlockSpec((tm, ZS), lambda i: (i, 0)),
                  pl.BlockSpec((tm, rt.shape[1]), lambda i: (i % nt, 0)),
                  _resident(gains.shape), _resident(g64.shape), _resident(g32.shape),
                  _resident(g96.shape), _resident(wuq.shape), _resident(wkc.shape),
                  _resident(wuv.shape)],
        out_specs=[pl.BlockSpec((tm, w), lambda i: (i, 0)) for w, _ in outs],
        out_shape=[jax.ShapeDtypeStruct((n, w), dt) for w, dt in outs],
        compiler_params=_cparams("parallel"),
        name="prep",
    )(zs, rt, gains, g64, g32, g96, wuq, wkc, wuv)


def _tflash(streams, m_ref, l_ref, acc_ref, first):
    scores = [_dot(k_blk, qt) + bias for k_blk, qt, bias, _, _ in streams]
    parts = []
    for s, (_, _, _, _, idx) in zip(scores, streams):
        m_cur = jnp.max(s, axis=0, keepdims=True)
        if first:
            m_new, alpha = m_cur, None
        else:
            m_prev = m_ref[idx]
            m_new = jnp.maximum(m_prev, m_cur)
            alpha = jnp.exp(m_prev - m_new)
        p = jnp.exp(s - m_new)
        parts.append((m_new, alpha, jnp.sum(p, axis=0, keepdims=True), p.astype(BF16)))
    pvs = [_dot(vt_blk, p16) for (_, _, _, vt_blk, _), (_, _, _, p16) in zip(streams, parts)]
    for (_, _, _, _, idx), (m_new, alpha, psum, _), pv in zip(streams, parts, pvs):
        m_ref[idx] = m_new
        if first:
            l_ref[idx] = psum
            acc_ref[idx] = pv
        else:
            l_ref[idx] = alpha * l_ref[idx] + psum
            acc_ref[idx] = alpha * acc_ref[idx] + pv


def _lanes(pieces):
    return pieces[0] if len(pieces) == 1 else jnp.concatenate(pieces, axis=-1)


def _moba_prompt_body(q_ref, kv_ref, ald_ref, al0_ref, sc_ref, avg_ref, o_ref,
                      k16, vt16, km_ref, selt_ref, m_ref, l_ref, acc_ref, *, nb):
    c = pl.program_id(1)
    hpk = A_HEADS // A_KV

    @pl.when(c == 0)
    def _():
        kv = kv_ref[0]
        kvt = kv.T
        for j in range(A_KV):
            k16[j] = kv[:, j * 128:j * 128 + 64].astype(BF16)
            vt16[j] = kvt[j * 128 + 64:(j + 1) * 128, :].astype(BF16)
        hi, lo = _split16(kvt)
        kmr = _dot(hi, avg_ref[...]) + _dot(lo, avg_ref[...])
        lane = lax.broadcasted_iota(jnp.int32, (64, 128), 1)
        rows = []
        for h in range(A_HEADS):
            j = h // hpk
            rows.append(jnp.where(lane // 32 == h, kmr[j * 128:j * 128 + 64, :], 0.0))
        hi, lo = _split16(jnp.concatenate(rows, axis=0))
        km_ref[0] = hi
        km_ref[1] = lo

    q = q_ref[...]
    bs = _dot(q, km_ref[0]) + _dot(q, km_ref[1])
    lane = lax.broadcasted_iota(jnp.int32, (QB, 128), 1)
    valid = (lane % 32) < c
    bsm = jnp.where(valid, bs, NEG)
    rank = jnp.zeros((QB, 128), F32)
    for k in range(1, nb):
        rank = rank + jnp.where(pltpu.roll(bsm, k, 1) >= bsm, 1.0, 0.0)
        rank = rank + jnp.where(pltpu.roll(bsm, 128 - k, 1) > bsm, 1.0, 0.0)
    selt_ref[...] = jnp.where(valid, jnp.where(rank < MOBA_TOPK, 1.0, 0.0), 0.0).T

    qt = q.astype(F32).T
    start = pl.multiple_of(c * QB, QB)
    qss = [_lanes([qt[(hpk * j + e) * 64:(hpk * j + e + 1) * 64, :] for e in range(hpk)]).astype(BF16)
           for j in range(A_KV)]
    _tflash([(k16[j, pl.ds(start, QB), :], qss[j], ald_ref[j], vt16[j, :, pl.ds(start, QB)], j)
             for j in range(A_KV)], m_ref, l_ref, acc_ref, True)

    def body(n, carry):
        off = pl.multiple_of(n * QB, QB)
        far = ((c - n) * QB).astype(F32)
        streams = []
        for j in range(A_KV):
            selrow = _lanes([selt_ref[pl.ds((hpk * j + e) * 32 + n, 1), :] for e in range(hpk)])
            bias = jnp.where(selrow > 0.5, al0_ref[j] - sc_ref[j] * far, NEG)
            streams.append((k16[j, pl.ds(off, QB), :], qss[j], bias, vt16[j, :, pl.ds(off, QB)], j))
        _tflash(streams, m_ref, l_ref, acc_ref, False)
        return carry

    lax.fori_loop(0, c, body, 0)
    outs = []
    for j in range(A_KV):
        ot = acc_ref[j] / l_ref[j]
        outs += [ot[:, e * QB:(e + 1) * QB] for e in range(hpk)]
    o_ref[...] = jnp.concatenate(outs, axis=0).T.astype(BF16)


def _moba_prompt(qa, a_new, consts, bsz, t):
    nq = t // QB
    nb = t // MOBA_BLOCK
    ald, al0, sc, avg = consts["a_ald"], consts["a_al0"], consts["a_sc"], consts["a_avg"]
    r = (A_HEADS // A_KV) * QB
    return pl.pallas_call(
        functools.partial(_moba_prompt_body, nb=nb),
        grid=(bsz, nq),
        in_specs=[pl.BlockSpec((QB, 256), lambda b, c: (b * nq + c, 0)),
                  pl.BlockSpec((1, t, 256), lambda b, c: (b, 0, 0)),
                  _resident(ald.shape), _resident(al0.shape), _resident(sc.shape),
                  _resident(avg.shape)],
        out_specs=pl.BlockSpec((QB, 256), lambda b, c: (b * nq + c, 0)),
        out_shape=jax.ShapeDtypeStruct((bsz * t, 256), BF16),
        scratch_shapes=[pltpu.VMEM((A_KV, t, 64), BF16), pltpu.VMEM((A_KV, 64, t), BF16),
                        pltpu.VMEM((2, 256, 128), BF16), pltpu.VMEM((128, QB), F32),
                        pltpu.VMEM((A_KV, 1, r), F32), pltpu.VMEM((A_KV, 1, r), F32),
                        pltpu.VMEM((A_KV, 64, r), F32)],
        compiler_params=_cparams("parallel", "arbitrary"),
        name="moba_prompt",
    )(qa, a_new.reshape(bsz, t, 256), ald, al0, sc, avg)


def _lam_of(lp, lam_init):
    a = jnp.sum(lp[0:1] * lp[1:2], axis=-1, keepdims=True)
    b = jnp.sum(lp[2:3] * lp[3:4], axis=-1, keepdims=True)
    return jnp.exp(a) - jnp.exp(b) + lam_init


def _diff_prompt_body(q_ref, kv_ref, lam_ref, gs_ref, ald_ref, al0_ref, sc_ref, o_ref,
                      k16, vt16, m_ref, l_ref, acc_ref, *, lam_init):
    c = pl.program_id(1)

    @pl.when(c == 0)
    def _():
        kv = kv_ref[0]
        for i in range(2):
            k16[i] = kv[:, i * 32:(i + 1) * 32].astype(BF16)
        vt16[...] = kv.T[64:128, :].astype(BF16)

    qt = q_ref[...].astype(F32).T
    start = pl.multiple_of(c * QB, QB)
    qss = [_lanes([qt[h * 64 + i * 32:h * 64 + (i + 1) * 32, :] for h in range(B_HEADS)]).astype(BF16)
           for i in range(2)]
    _tflash([(k16[i, pl.ds(start, QB), :], qss[i], ald_ref[...], vt16[:, pl.ds(start, QB)], i)
             for i in range(2)], m_ref, l_ref, acc_ref, True)

    def body(n, carry):
        off = pl.multiple_of(n * QB, QB)
        far = ((c - n) * QB).astype(F32)
        bias = al0_ref[...] - sc_ref[...] * far
        _tflash([(k16[i, pl.ds(off, QB), :], qss[i], bias, vt16[:, pl.ds(off, QB)], i)
                 for i in range(2)], m_ref, l_ref, acc_ref, False)
        return carry

    lax.fori_loop(0, c, body, 0)

    lam = _lam_of(lam_ref[...], lam_init)
    ot = acc_ref[0] / l_ref[0] - lam * (acc_ref[1] / l_ref[1])
    ot = ot * lax.rsqrt(jnp.mean(ot * ot, axis=0, keepdims=True) + EPS) * gs_ref[...] * (1.0 - lam_init)
    stacked = jnp.concatenate([ot[:, h * QB:(h + 1) * QB] for h in range(B_HEADS)], axis=0)
    o_ref[...] = stacked.T.astype(BF16)


def _diff_prompt(qb, b_new, b_lam, g_sub, consts, bsz, t, lam_init):
    nq = t // QB
    ald, al0, sc = consts["b_ald"], consts["b_al0"], consts["b_sc"]
    r = B_HEADS * QB
    g_col = g_sub.reshape(B_V, 1)
    return pl.pallas_call(
        functools.partial(_diff_prompt_body, lam_init=lam_init),
        grid=(bsz, nq),
        in_specs=[pl.BlockSpec((QB, 256), lambda b, c: (b * nq + c, 0)),
                  pl.BlockSpec((1, t, 128), lambda b, c: (b, 0, 0)),
                  _resident(b_lam.shape), _resident(g_col.shape),
                  _resident(ald.shape), _resident(al0.shape), _resident(sc.shape)],
        out_specs=pl.BlockSpec((QB, 256), lambda b, c: (b * nq + c, 0)),
        out_shape=jax.ShapeDtypeStruct((bsz * t, 256), BF16),
        scratch_shapes=[pltpu.VMEM((2, t, 32), BF16), pltpu.VMEM((B_V, t), BF16),
                        pltpu.VMEM((2, 1, r), F32), pltpu.VMEM((2, 1, r), F32),
                        pltpu.VMEM((2, B_V, r), F32)],
        compiler_params=_cparams("parallel", "arbitrary"),
        name="diff_prompt",
    )(qb, b_new.reshape(bsz, t, 128), b_lam, g_col, ald, al0, sc)


def _mla_prompt_body(q_ref, k_ref, v_ref, cm_ref, o_ref, vt16, m_ref, l_ref, acc_ref):
    c = pl.program_id(1)

    @pl.when(c == 0)
    def _():
        vt = v_ref[0].astype(F32).T
        for h in range(C_HEADS):
            vt16[h] = vt[h * C_V:(h + 1) * C_V, :].astype(BF16)

    start = pl.multiple_of(c * QB, QB)
    hss = [slice(h * CP, (h + 1) * CP) for h in range(C_HEADS)]
    qss = [q_ref[:, hs].astype(F32).T.astype(BF16) for hs in hss]
    _tflash([(k_ref[0, pl.ds(start, QB), hss[h]], qss[h], cm_ref[...], vt16[h, :, pl.ds(start, QB)], h)
             for h in range(C_HEADS)], m_ref, l_ref, acc_ref, True)

    def body(n, carry):
        off = pl.multiple_of(n * QB, QB)
        _tflash([(k_ref[0, pl.ds(off, QB), hss[h]], qss[h], 0.0, vt16[h, :, pl.ds(off, QB)], h)
                 for h in range(C_HEADS)], m_ref, l_ref, acc_ref, False)
        return carry

    lax.fori_loop(0, c, body, 0)
    outs = [acc_ref[h] / l_ref[h] for h in range(C_HEADS)]
    o_ref[...] = jnp.concatenate(outs, axis=0).T.astype(BF16)


def _mla_prompt(qc, kc, vc, consts, bsz, t):
    nq = t // QB
    cm = consts["c_cm"]
    return pl.pallas_call(
        _mla_prompt_body,
        grid=(bsz, nq),
        in_specs=[pl.BlockSpec((QB, CW), lambda b, c: (b * nq + c, 0)),
                  pl.BlockSpec((1, t, CW), lambda b, c: (b, 0, 0)),
                  pl.BlockSpec((1, t, 256), lambda b, c: (b, 0, 0)),
                  _resident(cm.shape)],
        out_specs=pl.BlockSpec((QB, 256), lambda b, c: (b * nq + c, 0)),
        out_shape=jax.ShapeDtypeStruct((bsz * t, 256), BF16),
        scratch_shapes=[pltpu.VMEM((C_HEADS, C_V, t), BF16),
                        pltpu.VMEM((C_HEADS, 1, QB), F32), pltpu.VMEM((C_HEADS, 1, QB), F32),
                        pltpu.VMEM((C_HEADS, C_V, QB), F32)],
        compiler_params=_cparams("parallel", "arbitrary"),
        name="mla_prompt",
    )(qc, kc.reshape(bsz, t, CW), vc.reshape(bsz, t, 256), cm)


def _pool_body(u_ref, prev_ref, cnt_ref, pw_ref, ps_ref, y_ref, ns_ref, buf, *, sb, t):
    u = u_ref[...]
    buf[:, 0:1, :] = jnp.zeros((sb, 1, D_POOL), F32)
    buf[:, 1:16, :] = prev_ref[...]
    buf[:, 16:16 + t, :] = u
    sums = {}
    run = u
    for k in range(1, 16):
        run = run + buf[:, 16 - k:16 - k + t, :]
        if k + 1 in POOL_WINDOWS:
            sums[k + 1] = run
    lane = lax.broadcasted_iota(jnp.int32, (sb, t, D_POOL), 2)
    tot = jnp.where(lane < 64, sums[2],
                    jnp.where(lane < 128, sums[4], jnp.where(lane < 192, sums[8], sums[16])))
    d = (tot / cnt_ref[...][None] - u).astype(BF16)
    y = _dot(d.reshape(sb * t, D_POOL), pw_ref[...]) * ps_ref[...]
    y_ref[...] = y.astype(BF16)
    ns_ref[...] = buf[:, t + 1:t + 16, :]


def _pool(zs, prev, cnt, pw, ps, sb, t):
    n = zs.shape[0]
    nseq = n // t
    return pl.pallas_call(
        functools.partial(_pool_body, sb=sb, t=t),
        grid=(nseq // sb,),
        in_specs=[pl.BlockSpec((sb, t, D_POOL), lambda i: (i, 0, O_DU // D_POOL)),
                  pl.BlockSpec((sb, POOL_STATE, D_POOL), lambda i: (i, 0, 0)),
                  _resident(cnt.shape), _resident(pw.shape), _resident(ps.shape)],
        out_specs=[pl.BlockSpec((sb * t, D_POOL), lambda i: (i, 0)),
                   pl.BlockSpec((sb, POOL_STATE, D_POOL), lambda i: (i, 0, 0))],
        out_shape=[jax.ShapeDtypeStruct((n, D_POOL), BF16),
                   jax.ShapeDtypeStruct((nseq, POOL_STATE, D_POOL), F32)],
        scratch_shapes=[pltpu.VMEM((sb, t + 16, D_POOL), F32)],
        compiler_params=_cparams("parallel"),
        name="pool",
    )(zs.reshape(nseq, t, ZS), prev, cnt, pw, ps)


def _merge_body(x_ref, a_ref, b_ref, c_ref, d_ref, gl_ref, wb_ref, wo_ref, o_ref):
    merged = None
    for n, r in enumerate((a_ref, b_ref, c_ref, d_ref)):
        gate = jax.nn.sigmoid(gl_ref[:, n * D_MODEL:(n + 1) * D_MODEL])
        term = gate * _dot(r[...], wb_ref[n])
        merged = term if merged is None else merged + term
    o_ref[...] = x_ref[...] + _dot(merged.astype(BF16), wo_ref[...])


def _merge(x, oa, ob, oc, od, gl, wb, wo, tm):
    n = x.shape[0]
    br = pl.BlockSpec((tm, 256), lambda i: (i, 0))
    return pl.pallas_call(
        _merge_body,
        grid=(n // tm,),
        in_specs=[pl.BlockSpec((tm, D_MODEL), lambda i: (i, 0)), br, br, br, br,
                  pl.BlockSpec((tm, 4 * D_MODEL), lambda i: (i, 0)),
                  _resident(wb.shape), _resident(wo.shape)],
        out_specs=pl.BlockSpec((tm, D_MODEL), lambda i: (i, 0)),
        out_shape=jax.ShapeDtypeStruct((n, D_MODEL), F32),
        compiler_params=_cparams("parallel"),
        name="merge",
    )(x, oa, ob, oc, od, gl, wb, wo)


FC = 256
NF = D_FF // FC


def _ffn_body(h_ref, st_ref, g_ref, wup_ref, wdn_ref, cw_ref, cb_ref, o_ref, ns_ref,
              buf, carry, act_ref, *, sb, t, tiles_per_seq):
    i = pl.program_id(0)
    first = (i % tiles_per_seq) == 0
    h = h_ref[...]
    hn = (h * lax.rsqrt(jnp.mean(h * h, axis=-1, keepdims=True) + EPS) * g_ref[...]).astype(BF16)
    def up(f):
        ga = _dot(hn, wup_ref[:, f * FC:(f + 1) * FC]).reshape(sb, t, FC)
        va = _dot(hn, wup_ref[:, D_FF + f * FC:D_FF + (f + 1) * FC]).reshape(sb, t, FC)
        return ga, va

    nxt = up(0)
    for f in range(NF):
        sl = slice(f * FC, (f + 1) * FC)
        ga, va = nxt
        if f + 1 < NF:
            nxt = up(f + 1)

        @pl.when(first)
        def _(sl=sl):
            buf[:, 6:8, :] = st_ref[:, :, sl]

        @pl.when(jnp.logical_not(first))
        def _(f=f):
            buf[:, 6:8, :] = carry[f]

        buf[:, 8:8 + t, :] = ga
        conv = (cb_ref[:, sl][None] + buf[:, 6:6 + t, :] * cw_ref[0:1, sl][None]
                + buf[:, 7:7 + t, :] * cw_ref[1:2, sl][None] + ga * cw_ref[2:3, sl][None])
        act_ref[:, sl] = (conv * jax.nn.sigmoid(conv) * va).reshape(sb * t, FC).astype(BF16)
        last = buf[:, 6 + t:8 + t, :]
        carry[f] = last
        ns_ref[:, :, sl] = last
    o_ref[...] = h + _dot(act_ref[...], wdn_ref[...])


def _ffn(h, st, g, wup, wdn, cw, cb, sb, t, tiles_per_seq):
    n = h.shape[0]
    tm = sb * t
    nst = st.shape[0]
    return pl.pallas_call(
        functools.partial(_ffn_body, sb=sb, t=t, tiles_per_seq=tiles_per_seq),
        grid=(n // tm,),
        in_specs=[pl.BlockSpec((tm, D_MODEL), lambda i: (i, 0)),
                  pl.BlockSpec((sb, 2, D_FF), lambda i: (i // tiles_per_seq, 0, 0)),
                  _resident((1, D_MODEL)), _resident(wup.shape), _resident(wdn.shape),
                  _resident(cw.shape), _resident(cb.shape)],
        out_specs=[pl.BlockSpec((tm, D_MODEL), lambda i: (i, 0)),
                   pl.BlockSpec((sb, 2, D_FF), lambda i: (i // tiles_per_seq, 0, 0))],
        out_shape=[jax.ShapeDtypeStruct((n, D_MODEL), F32),
                   jax.ShapeDtypeStruct((nst, 2, D_FF), F32)],
        scratch_shapes=[pltpu.VMEM((sb, t + 8, FC), F32), pltpu.VMEM((NF, sb, 2, FC), F32),
                        pltpu.VMEM((tm, D_FF), BF16)],
        compiler_params=_cparams("arbitrary"),
        name="ffn",
    )(h, st, g, wup, wdn, cw, cb)


def _page_specs(cache, layer, gs, pp):
    tail = cache.shape[2:]
    zeros = (0,) * len(tail)
    return [pl.BlockSpec((None, None) + tail,
                         lambda b, g, pt, s=s, k=k: (layer, pt[b * gs + s, g * pp + k]) + zeros)
            for s in range(gs) for k in range(pp)]


def _sres(shape):
    nd = len(shape)
    return pl.BlockSpec(shape, lambda b, g, pt: (0,) * nd, pipeline_mode=pl.Buffered(1))


def _pad_rows(x, n):
    return jnp.concatenate([x, jnp.zeros((n - x.shape[0], x.shape[1]), x.dtype)], axis=0)


def _moba_sample_body(pt_ref, q_ref, new_ref, place_ref, als_ref, ald_ref, sc_ref, *rest,
                      gs, pp, ts, past_len, nblk):
    pages = rest[:gs * pp]
    o_ref = rest[gs * pp]
    qbd, m_ref, l_ref, acc_ref, ks_ref = rest[gs * pp + 1:]
    g = pl.program_id(1)
    rows = A_HEADS * ts
    hpk = A_HEADS // A_KV
    ppb = MOBA_BLOCK // PAGE
    bps = pp // ppb
    vk = A_KV * PAGE
    lane = lax.broadcasted_iota(jnp.int32, (rows, 128), 1)

    @pl.when(g == 0)
    def _():
        for s in range(gs):
            q = q_ref[s * ts:(s + 1) * ts, :]
            for h in range(A_HEADS):
                qbd[s, h * ts:(h + 1) * ts, :] = _dot(q, place_ref[h]).astype(BF16)
        m_ref[...] = jnp.zeros(m_ref.shape, F32)
        l_ref[...] = jnp.zeros(l_ref.shape, F32)

    units = [(s, nl) for s in range(gs) for nl in range(bps)]
    r16s, scs = {}, {}
    for s, nl in units:
        r16s[s, nl] = [pages[s * pp + nl * ppb + e][...].astype(BF16) for e in range(ppb)]
        ss = []
        for e in range(ppb):
            far = (past_len - (g * pp + nl * ppb + e) * PAGE).astype(F32)
            ss.append(_dot_nt(qbd[s], r16s[s, nl][e]) + als_ref[...] - sc_ref[...] * far)
        scs[s, nl] = jnp.concatenate(ss, axis=-1)
    stats = {}
    for u in units:
        m = jnp.max(scs[u], axis=-1, keepdims=True)
        p = jnp.exp(scs[u] - m).astype(BF16)
        stats[u] = (m, jnp.sum(p.astype(F32), axis=-1, keepdims=True), p)
    accs = {}
    for u in units:
        p = stats[u][2]
        acc = _dot(p[:, 0:vk], r16s[u][0])
        for e in range(1, ppb):
            acc = acc + _dot(p[:, e * vk:(e + 1) * vk], r16s[u][e])
        accs[u] = acc
    for s in range(gs):
        mm = m_ref[s]
        ll = l_ref[s]
        ksums = []
        for nl in range(bps):
            blk = g * bps + nl
            mm = jnp.where(lane == blk, stats[s, nl][0], mm)
            ll = jnp.where(lane == blk, stats[s, nl][1], ll)
            ksum = jnp.sum(pages[s * pp + nl * ppb][...].reshape(vk // 8, 8, 128), axis=0)
            for e in range(1, ppb):
                ksum = ksum + jnp.sum(pages[s * pp + nl * ppb + e][...].reshape(vk // 8, 8, 128), axis=0)
            ksums.append(ksum)
        acc_ref[s, pl.ds(g * bps, bps)] = jnp.stack([accs[s, nl] for nl in range(bps)])
        ks_ref[s, pl.ds(g * bps, bps)] = jnp.stack(ksums)
        m_ref[s] = mm
        l_ref[s] = ll

    @pl.when(g == pl.num_programs(1) - 1)
    def _():
        sub = lax.broadcasted_iota(jnp.int32, (nblk, 8, 128), 1)
        row_kv = lax.broadcasted_iota(jnp.int32, (rows, 128), 0) // (ts * hpk)
        bss, new16s, s_owns = [], [], []
        for s in range(gs):
            ks = ks_ref[s]
            bs = jnp.zeros((rows, 128), F32)
            for j in range(A_KV):
                kmean = jnp.sum(jnp.where(sub % A_KV == j, ks, 0.0), axis=1) * (1.0 / MOBA_BLOCK)
                hi, lo = _split16(_pad_rows(kmean, 128))
                bs = jnp.where(row_kv == j, _dot_nt(qbd[s], hi) + _dot_nt(qbd[s], lo), bs)
            bss.append(bs)
            new16s.append(_pad_rows(new_ref[s * ts * A_KV:(s + 1) * ts * A_KV, :], PAGE).astype(BF16))
            s_owns.append(_dot_nt(qbd[s], new16s[s]) + ald_ref[...])
        parts = []
        for s in range(gs):
            bs = bss[s]
            rank = jnp.zeros((rows, 128), F32)
            for mblk in range(nblk):
                col = bs[:, mblk:mblk + 1]
                rank = rank + jnp.where(col > bs, 1.0, jnp.where((col == bs) & (lane > mblk), 1.0, 0.0))
            sel = (lane < nblk) & (rank < MOBA_TOPK)
            m_own = jnp.max(s_owns[s], axis=-1, keepdims=True)
            p_own = jnp.exp(s_owns[s] - m_own).astype(BF16)
            mf = jnp.maximum(m_own, jnp.max(jnp.where(sel, m_ref[s], NEG), axis=-1, keepdims=True))
            wgt = jnp.where(sel, jnp.exp(m_ref[s] - mf), 0.0)
            w_own = jnp.exp(m_own - mf)
            lsum = (w_own * jnp.sum(p_own.astype(F32), axis=-1, keepdims=True)
                    + jnp.sum(wgt * l_ref[s], axis=-1, keepdims=True))
            parts.append((p_own, w_own, wgt, lsum))
        pv_owns = [_dot(parts[s][0], new16s[s]) for s in range(gs)]
        for s in range(gs):
            _, w_own, wgt, lsum = parts[s]
            acc = w_own * pv_owns[s]
            for n in range(nblk):
                acc = acc + wgt[:, n:n + 1] * acc_ref[s, n]
            o = acc / lsum
            for h in range(A_HEADS):
                o_ref[s * ts:(s + 1) * ts, h * 64:(h + 1) * 64] = (
                    o[h * ts:(h + 1) * ts, 64:128].astype(BF16))


def _sample_grid(bs, n_pages, gs, pp, q_spec, new_spec, consts_specs, cache, layer, scratch):
    return pltpu.PrefetchScalarGridSpec(
        num_scalar_prefetch=1,
        grid=(bs // gs, n_pages // pp),
        in_specs=[pl.BlockSpec(q_spec, lambda b, g, pt: (b, 0)),
                  pl.BlockSpec(new_spec, lambda b, g, pt: (b, 0))]
        + consts_specs + _page_specs(cache, layer, gs, pp),
        out_specs=pl.BlockSpec((q_spec[0], 256), lambda b, g, pt: (b, 0)),
        scratch_shapes=scratch,
    )


def _moba_sample(qa, a_new, cache, layer, page_table, consts, bs, ts, gs, pp):
    n_pages = page_table.shape[1]
    past_len = n_pages * PAGE
    nblk = past_len // MOBA_BLOCK
    assert nblk <= 128 and A_KV * ts <= PAGE
    rows = A_HEADS * ts
    cs = [consts[k] for k in ("sa_place", "sa_als", "sa_ald", "sa_sc")]
    cache2 = cache.reshape(cache.shape[0], cache.shape[1], A_KV * PAGE, 128)
    grid_spec = _sample_grid(
        bs, n_pages, gs, pp, (gs * ts, 256), (gs * ts * A_KV, 128), [_sres(c.shape) for c in cs],
        cache2, layer,
        [pltpu.VMEM((gs, rows, 128), BF16), pltpu.VMEM((gs, rows, 128), F32),
         pltpu.VMEM((gs, rows, 128), F32), pltpu.VMEM((gs, nblk, rows, 128), F32),
         pltpu.VMEM((gs, nblk, 8, 128), F32)])
    return pl.pallas_call(
        functools.partial(_moba_sample_body, gs=gs, pp=pp, ts=ts, past_len=past_len, nblk=nblk),
        grid_spec=grid_spec,
        out_shape=jax.ShapeDtypeStruct((bs * ts, 256), BF16),
        compiler_params=_cparams("parallel", "arbitrary"),
        name="moba_sample",
    )(page_table, qa, a_new.reshape(bs * ts * A_KV, 128), *cs, *([cache2] * (gs * pp)))


def _softmax_step(s, pv_fn, m_ref, l_ref, acc_ref, idx):
    m_prev = m_ref[idx]
    m_new = jnp.maximum(m_prev, jnp.max(s, axis=-1, keepdims=True))
    alpha = jnp.exp(m_prev - m_new)
    p = jnp.exp(s - m_new).astype(BF16)
    m_ref[idx] = m_new
    l_ref[idx] = alpha * l_ref[idx] + jnp.sum(p.astype(F32), axis=-1, keepdims=True)
    acc_ref[idx] = alpha * acc_ref[idx] + pv_fn(p)


def _softmax_multi(scores, pv_fns, m_ref, l_ref, acc_ref):
    parts = []
    for idx, s in enumerate(scores):
        m_prev = m_ref[idx]
        m_new = jnp.maximum(m_prev, jnp.max(s, axis=-1, keepdims=True))
        p = jnp.exp(s - m_new).astype(BF16)
        parts.append((m_new, jnp.exp(m_prev - m_new), jnp.sum(p.astype(F32), axis=-1, keepdims=True), p))
    pvs = [fn(part[3]) for fn, part in zip(pv_fns, parts)]
    for idx, ((m_new, alpha, psum, _), pv) in enumerate(zip(parts, pvs)):
        m_ref[idx] = m_new
        l_ref[idx] = alpha * l_ref[idx] + psum
        acc_ref[idx] = alpha * acc_ref[idx] + pv


def _pv_pages(p, r16):
    n = r16[0].shape[0]
    pv = _dot(p[:, 0:n], r16[0])
    for k in range(1, len(r16)):
        pv = pv + _dot(p[:, k * n:(k + 1) * n], r16[k])
    return pv


def _diff_sample_body(pt_ref, q_ref, new_ref, place_ref, lam_ref, gs_ref, als_ref, ald_ref, sc_ref,
                      *rest, gs, pp, ts, past_len, lam_init):
    pages = rest[:gs * pp]
    o_ref = rest[gs * pp]
    qbd, m_ref, l_ref, acc_ref = rest[gs * pp + 1:]
    g = pl.program_id(1)
    half = B_HEADS * ts

    @pl.when(g == 0)
    def _():
        for s in range(gs):
            q = q_ref[s * ts:(s + 1) * ts, :]
            for e in range(2 * B_HEADS):
                qbd[s, e * ts:(e + 1) * ts, :] = _dot(q, place_ref[e]).astype(BF16)
        m_ref[...] = jnp.full(m_ref.shape, NEG, F32)
        l_ref[...] = jnp.zeros(l_ref.shape, F32)
        acc_ref[...] = jnp.zeros(acc_ref.shape, F32)

    r16s = [[jnp.concatenate([pages[s * pp + k + e][...].astype(BF16) for e in range(2)], axis=0)
             for k in range(0, pp, 2)] for s in range(gs)]
    scores = []
    for s in range(gs):
        ss = []
        for kk in range(pp // 2):
            far = (past_len - (g * pp + 2 * kk) * PAGE).astype(F32)
            ss.append(_dot_nt(qbd[s], r16s[s][kk]) + als_ref[...] - sc_ref[...] * far)
        scores.append(jnp.concatenate(ss, axis=-1))
    _softmax_multi(scores, [functools.partial(_pv_pages, r16=r16s[s]) for s in range(gs)],
                   m_ref, l_ref, acc_ref)

    @pl.when(g == pl.num_programs(1) - 1)
    def _():
        lam = _lam_of(lam_ref[...], lam_init)
        for s in range(gs):
            new16 = _pad_rows(new_ref[s * ts:(s + 1) * ts, :], PAGE).astype(BF16)
            s_own = _dot_nt(qbd[s], new16) + ald_ref[...]
            _softmax_step(s_own, lambda p, new16=new16: _dot(p, new16), m_ref, l_ref, acc_ref, s)
            o = acc_ref[s] / l_ref[s]
            o = o[0:half, 64:128] - lam * o[half:2 * half, 64:128]
            o = (o * lax.rsqrt(jnp.mean(o * o, axis=-1, keepdims=True) + EPS) * gs_ref[...]
                 * (1.0 - lam_init))
            for h in range(B_HEADS):
                o_ref[s * ts:(s + 1) * ts, h * 64:(h + 1) * 64] = o[h * ts:(h + 1) * ts].astype(BF16)


def _diff_sample(qb, b_new, b_lam, g_sub, cache, layer, page_table, consts, bs, ts, gs, pp, lam_init):
    n_pages = page_table.shape[1]
    past_len = n_pages * PAGE
    rows = 2 * B_HEADS * ts
    cs = [consts["sb_place"], b_lam, g_sub, consts["sb_als"], consts["sb_ald"], consts["sb_sc"]]
    cache2 = cache.reshape(cache.shape[0], cache.shape[1], PAGE, 128)
    grid_spec = _sample_grid(
        bs, n_pages, gs, pp, (gs * ts, 256), (gs * ts, 128), [_sres(c.shape) for c in cs], cache2, layer,
        [pltpu.VMEM((gs, rows, 128), BF16), pltpu.VMEM((gs, rows, 1), F32),
         pltpu.VMEM((gs, rows, 1), F32), pltpu.VMEM((gs, rows, 128), F32)])
    return pl.pallas_call(
        functools.partial(_diff_sample_body, gs=gs, pp=pp, ts=ts, past_len=past_len,
                          lam_init=lam_init),
        grid_spec=grid_spec,
        out_shape=jax.ShapeDtypeStruct((bs * ts, 256), BF16),
        compiler_params=_cparams("parallel", "arbitrary"),
        name="diff_sample",
    )(page_table, qb, b_new, *cs, *([cache2] * (gs * pp)))


def _mla_sample_body(pt_ref, q_ref, new_ref, mq_ref, wext_ref, wuv_ref, cm_ref, *rest, gs, pp, ts):
    pages = rest[:gs * pp]
    o_ref = rest[gs * pp]
    lhs, pbuf, nbuf, m_ref, l_ref, acc_ref = rest[gs * pp + 1:]
    b = pl.program_id(0)
    g = pl.program_id(1)
    rows = C_HEADS * ts
    next_rows = C_HEADS * C_NOPE + C_ROPE
    width = C_KVL + C_ROPE

    @pl.when((b == 0) & (g == 0))
    def _():
        pbuf[...] = jnp.zeros(pbuf.shape, BF16)
        nbuf[...] = jnp.zeros(nbuf.shape, F32)
        for s in range(gs):
            lhs[s, rows:rows + next_rows, :] = wext_ref[...]

    @pl.when(g == 0)
    def _():
        for s in range(gs):
            q = q_ref[s * ts:(s + 1) * ts, :]
            for h in range(C_HEADS):
                lhs[s, h * ts:(h + 1) * ts, :] = _dot(q, mq_ref[h]).astype(BF16)
        m_ref[...] = jnp.full(m_ref.shape, NEG, F32)
        l_ref[...] = jnp.zeros(l_ref.shape, F32)
        acc_ref[...] = jnp.zeros(acc_ref.shape, F32)

    def scores(s, kt16):
        nk = kt16.shape[1]
        out = _dot(lhs[s], kt16)
        kt = out[rows:rows + next_rows]
        kt2 = kt * kt
        ssq = jnp.sum(kt2[0:C_HEADS * C_NOPE].reshape(C_HEADS, C_NOPE, nk), axis=1)
        ssq = ssq + jnp.sum(kt2[C_HEADS * C_NOPE:], axis=0, keepdims=True)
        r = lax.rsqrt(ssq * (1.0 / C_QK) + EPS)
        r_rows = jnp.broadcast_to(r[:, None, :], (C_HEADS, ts, nk)).reshape(rows, nk)
        return out[0:rows] * r_rows

    def pv_pages(p, s):
        pv = _dot_nt(p[:, 0:2 * PAGE], pbuf[s, 0, 0:C_KVL, :])
        for kk in range(1, pp // 2):
            pv = pv + _dot_nt(p[:, kk * 2 * PAGE:(kk + 1) * 2 * PAGE], pbuf[s, kk, 0:C_KVL, :])
        return pv

    all_scores = []
    for s in range(gs):
        ss = []
        for kk in range(pp // 2):
            for e in range(2):
                pbuf[s, kk, 0:width, e * PAGE:(e + 1) * PAGE] = pages[s * pp + 2 * kk + e][...].astype(BF16)
            ss.append(scores(s, pbuf[s, kk]))
        all_scores.append(jnp.concatenate(ss, axis=-1))
    _softmax_multi(all_scores, [functools.partial(pv_pages, s=s) for s in range(gs)],
                   m_ref, l_ref, acc_ref)

    @pl.when(g == pl.num_programs(1) - 1)
    def _():
        for s in range(gs):
            nbuf[0:ts, 0:width] = new_ref[s * ts:(s + 1) * ts, :]
            nkt = nbuf[...].T.astype(BF16)
            _softmax_step(scores(s, nkt) + cm_ref[...],
                          lambda p, nkt=nkt: _dot_nt(p, nkt[0:C_KVL, :]), m_ref, l_ref, acc_ref, s)
            o_lat = (acc_ref[s] / l_ref[s]).astype(BF16)
            for h in range(C_HEADS):
                o_ref[s * ts:(s + 1) * ts, h * C_V:(h + 1) * C_V] = _dot(
                    o_lat[h * ts:(h + 1) * ts], wuv_ref[:, h * C_V:(h + 1) * C_V]).astype(BF16)


def _mla_sample(qce, c_new, mq, wext, wuv, cache, layer, page_table, consts, bs, ts, gs, pp):
    n_pages = page_table.shape[1]
    rows = C_HEADS * ts
    cs = [mq, wext, wuv, consts["sc_cm"]]
    cache_t = jnp.swapaxes(cache, 2, 3)
    grid_spec = _sample_grid(
        bs, n_pages, gs, pp, (gs * ts, CW), (gs * ts, 160), [_sres(c.shape) for c in cs], cache_t, layer,
        [pltpu.VMEM((gs, rows + wext.shape[0], 256), BF16),
         pltpu.VMEM((gs, pp // 2, 256, 2 * PAGE), BF16), pltpu.VMEM((PAGE, 256), F32),
         pltpu.VMEM((gs, rows, 1), F32), pltpu.VMEM((gs, rows, 1), F32),
         pltpu.VMEM((gs, rows, C_KVL), F32)])
    return pl.pallas_call(
        functools.partial(_mla_sample_body, gs=gs, pp=pp, ts=ts),
        grid_spec=grid_spec,
        out_shape=jax.ShapeDtypeStruct((bs * ts, 256), BF16),
        compiler_params=_cparams("arbitrary", "arbitrary"),
        name="mla_sample",
    )(page_table, qce, c_new, *cs, *([cache_t] * (gs * pp)))


def _alibi(n):
    return np.asarray(2.0 ** (-8.0 * np.arange(1, n + 1) / n), dtype=np.float32)


def _group_matrix(width, group):
    idx = np.arange(width) // group
    return jnp.asarray(idx[:, None] == idx[None, :], dtype=BF16)


def _rope_tables(pos, reps):
    half = C_ROPE // 2
    inv = ROPE_THETA ** (-jnp.arange(half, dtype=F32) / half)
    ang = pos.astype(F32)[:, None] * inv[None, :]
    cos, sin = jnp.cos(ang), jnp.sin(ang)
    n = pos.shape[0]
    one, zero = jnp.ones((n, C_NOPE), F32), jnp.zeros((n, C_NOPE), F32)
    zh = jnp.zeros((n, half), F32)
    zp = jnp.zeros((n, CP - C_QK), F32)
    qcos = jnp.concatenate([one, cos, cos, zp] * C_HEADS, axis=-1)
    qsa = jnp.concatenate([zero, -sin, zh, zp] * C_HEADS, axis=-1)
    qsb = jnp.concatenate([zero, zh, sin, zp] * C_HEADS, axis=-1)
    pad = jnp.zeros((n, 128 - C_ROPE), F32)
    kcos = jnp.concatenate([cos, cos, pad], axis=-1)
    ksa = jnp.concatenate([-sin, zh, pad], axis=-1)
    ksb = jnp.concatenate([zh, sin, pad], axis=-1)
    rt = jnp.concatenate([qcos, qsa, qsb, kcos, ksa, ksb], axis=-1)
    return jnp.tile(rt, (reps, 1))


def _pool_counts(pos):
    win = np.repeat(np.asarray(POOL_WINDOWS), POOL_GROUP)[None, :]
    return jnp.asarray(np.minimum(win, np.asarray(pos)[:, None] + 1), dtype=F32)


def _make_consts(tp, ts, past_len, tm_s):
    c = {}
    sl = _alibi(A_HEADS)
    i = np.arange(QB)[:, None]
    r = np.arange(QB)[None, :]
    dist = (i - r).astype(np.float32)
    causal = i >= r

    def alibi_tables(heads):
        al0 = np.concatenate([-sl[h] * dist.T for h in heads], axis=1)
        ald = np.concatenate([np.where(causal.T, -sl[h] * dist.T, NEG) for h in heads], axis=1)
        sc = np.concatenate([np.full((1, QB), sl[h], np.float32) for h in heads], axis=1)
        return ald.astype(np.float32), al0.astype(np.float32), sc

    hpk = A_HEADS // A_KV
    per = [alibi_tables(range(j * hpk, (j + 1) * hpk)) for j in range(A_KV)]
    c["a_ald"], c["a_al0"], c["a_sc"] = (jnp.asarray(np.stack([p[k] for p in per])) for k in range(3))
    c["b_ald"], c["b_al0"], c["b_sc"] = (jnp.asarray(x) for x in alibi_tables(range(B_HEADS)))
    c["c_cm"] = jnp.asarray(np.where(causal.T, 0.0, NEG).astype(np.float32))
    nb = tp // MOBA_BLOCK
    assert nb + MOBA_TOPK <= 32 and tp % QB == 0
    t_idx = np.arange(tp)[:, None] // MOBA_BLOCK
    lane = np.arange(128)[None, :]
    c["a_avg"] = jnp.asarray(np.where(t_idx == lane % 32, 1.0 / MOBA_BLOCK, 0.0), dtype=BF16)
    c["rt_p"] = _rope_tables(jnp.arange(tp, dtype=jnp.int32), 1)
    c["rt_s"] = _rope_tables(past_len + jnp.arange(ts, dtype=jnp.int32), tm_s // ts)
    c["cnt_p"] = _pool_counts(np.arange(tp))
    c["cnt_s"] = _pool_counts(past_len + np.arange(ts))
    c["g64"], c["g32"], c["g96"] = _group_matrix(256, 64), _group_matrix(256, 32), _group_matrix(CW, CP)

    assert past_len % MOBA_BLOCK == 0 and ts <= PAGE and MOBA_BLOCK % PAGE == 0
    qi = np.tile(np.arange(ts), A_HEADS)[:, None]
    kr = np.arange(PAGE)[None, :]
    slr = np.repeat(sl, ts)[:, None]
    als = (-slr * (qi - kr)).astype(np.float32)
    own_ok = (kr <= qi) & (kr < ts)
    ald = np.where(own_ok, als, NEG).astype(np.float32)
    als2 = (-slr * (qi - np.arange(2 * PAGE)[None, :])).astype(np.float32)
    c["sb_als"], c["sb_ald"], c["sb_sc"] = (jnp.asarray(np.concatenate([x, x], axis=0))
                                            for x in (als2, ald, slr))
    c["sc_cm"] = jnp.asarray(np.where(own_ok, 0.0, NEG).astype(np.float32))
    vr = np.arange(A_KV * PAGE)[None, :]
    row_kv = (np.repeat(np.arange(A_HEADS), ts) // hpk)[:, None]
    mine = (vr % A_KV) == row_kv
    als_a = np.where(mine, -slr * (qi - vr // A_KV), NEG).astype(np.float32)
    vo = np.arange(PAGE)[None, :]
    own_a = ((vo % A_KV) == row_kv) & (vo // A_KV <= qi) & (vo // A_KV < ts)
    ald_a = np.where(own_a, -slr * (qi - vo // A_KV), NEG).astype(np.float32)
    c["sa_als"], c["sa_ald"], c["sa_sc"] = jnp.asarray(als_a), jnp.asarray(ald_a), jnp.asarray(slr)
    pa = np.zeros((A_HEADS, 256, 128), np.float32)
    for h in range(A_HEADS):
        pa[h, h * 64 + np.arange(64), np.arange(64)] = 1.0
    c["sa_place"] = jnp.asarray(pa, dtype=BF16)
    pb = np.zeros((2 * B_HEADS, 256, 128), np.float32)
    for e in range(2):
        for h in range(B_HEADS):
            pb[e * B_HEADS + h, h * 64 + e * 32 + np.arange(32), e * 32 + np.arange(32)] = 1.0
    c["sb_place"] = jnp.asarray(pb, dtype=BF16)
    return c


def _layer_weights(l, p):
    w = {}
    w_in = p["w_in"][l]
    sizes = (256, 128, 128, 256, 64, 64, 256, 160, 256, 4096)
    offs = np.concatenate([[0], np.cumsum(sizes)])
    aq, ak, av, bq, bk, bv, cq, ckv, du, gl = (w_in[:, offs[k]:offs[k + 1]] for k in range(10))
    akv = jnp.concatenate([ak[:, 0:64], av[:, 0:64], ak[:, 64:128], av[:, 64:128]], axis=1)
    pad = jnp.zeros((D_MODEL, 96), F32)
    w["ws"] = jnp.concatenate([aq, akv, bq, cq, du, ckv, pad, bk, bv], axis=1).astype(BF16)
    w["wg"] = gl.astype(BF16)
    w["g1"] = p["norm1_g"][l][None]
    w["g2"] = p["norm2_g"][l][None]

    def row(v):
        return jnp.pad(v, (0, CW - v.shape[0]))[None]

    def heads_padded(v):
        return jnp.tile(jnp.pad(v, (0, CP - C_QK)), C_HEADS)

    one64 = jnp.ones((64,), F32)
    zero64 = jnp.zeros((64,), F32)
    gains = [
        row(jnp.tile(p["a_q_g"][l], 4) * (A_DIM ** -0.5)),
        row(jnp.concatenate([p["a_k_g"][l], one64] * 2)),
        row(jnp.concatenate([one64, zero64] * 2)),
        row(jnp.tile(p["b_q_g"][l], 8) * (B_QK ** -0.5)),
        row(jnp.concatenate([p["b_k_g"][l], p["b_k_g"][l], one64])),
        row(jnp.concatenate([one64, zero64])),
        row(p["c_qa_g"][l]),
        row(heads_padded(p["c_q_g"][l]) * (C_QK ** -0.5)),
        row(p["c_kva_g"][l]),
        row(heads_padded(p["c_k_g"][l])),
    ]
    w["gains"] = jnp.concatenate(gains + [jnp.zeros((16 - len(gains), CW), F32)], axis=0)
    w["wuq"] = jnp.pad(p["w_uq"][l], ((0, 0), (0, 0), (0, CP - C_QK))).reshape(C_QL, CW).astype(BF16)
    w_uk = p["w_uk"][l]
    eye = jnp.eye(C_ROPE, dtype=F32)
    top = jnp.concatenate([jnp.concatenate([w_uk[:, h], jnp.zeros((C_KVL, CP - C_NOPE), F32)], axis=1)
                           for h in range(C_HEADS)], axis=1)
    mid = jnp.concatenate([jnp.concatenate([jnp.zeros((C_ROPE, C_NOPE), F32), eye,
                                            jnp.zeros((C_ROPE, CP - C_QK), F32)], axis=1)
                           for _ in range(C_HEADS)], axis=1)
    w["wkc"] = jnp.concatenate([top, mid, jnp.zeros((96, CW), F32)], axis=0).astype(BF16)
    w["wuv"] = p["w_uv"][l].reshape(C_KVL, C_HEADS * C_V).astype(BF16)
    mq = []
    for h in range(C_HEADS):
        m = jnp.zeros((CW, 256), F32)
        m = m.at[h * CP:h * CP + C_NOPE, 0:C_KVL].set(w_uk[:, h].T)
        m = m.at[h * CP + C_NOPE:h * CP + C_QK, C_KVL:C_KVL + C_ROPE].set(eye)
        mq.append(m)
    w["mq"] = jnp.stack(mq).astype(BF16)
    probe = jnp.concatenate([w_uk[:, h].T for h in range(C_HEADS)], axis=0)
    probe = jnp.concatenate([probe, jnp.zeros((C_HEADS * C_NOPE, 128), F32)], axis=1)
    rope_rows = jnp.zeros((C_ROPE, 256), F32).at[:, C_KVL:C_KVL + C_ROPE].set(eye)
    w["wext"] = jnp.concatenate([probe, rope_rows], axis=0).astype(BF16)
    pw = jnp.zeros((D_POOL, D_POOL), F32)
    for gi in range(len(POOL_WINDOWS)):
        sl = slice(gi * POOL_GROUP, (gi + 1) * POOL_GROUP)
        pw = pw.at[sl, sl].set(p["pool_w"][l][gi])
    w["pw"] = pw.astype(BF16)
    w["ps"] = p["pool_scale"][l][None]
    w["wb"] = p["w_branch"][l].astype(BF16)
    w["wo"] = p["w_out"][l].astype(BF16)
    w["wup"] = p["w_up"][l].astype(BF16)
    w["wdn"] = p["w_down"][l].astype(BF16)
    w["cw"] = p["conv_w"][l]
    w["cb"] = p["conv_b"][l][None]
    w["b_lam"] = p["b_lam"][l]
    w["g_sub"] = p["b_sub_g"][l][None]
    return w


def _tile(n, pref):
    return pref if n % pref == 0 else n


def kernel(x_prompt, x_sample, cache_a, cache_b, cache_c, state_pool, state_conv, page_table,
           norm1_g, w_in, a_q_g, a_k_g, b_q_g, b_k_g, b_lam, b_sub_g, c_qa_g, c_kva_g,
           w_uq, w_uk, w_uv, c_q_g, c_k_g, pool_w, pool_scale, w_branch, w_out,
           norm2_g, w_up, conv_w, conv_b, w_down):
    params = dict(norm1_g=norm1_g, w_in=w_in, a_q_g=a_q_g, a_k_g=a_k_g, b_q_g=b_q_g, b_k_g=b_k_g,
                  b_lam=b_lam, b_sub_g=b_sub_g, c_qa_g=c_qa_g, c_kva_g=c_kva_g, w_uq=w_uq,
                  w_uk=w_uk, w_uv=w_uv, c_q_g=c_q_g, c_k_g=c_k_g, pool_w=pool_w,
                  pool_scale=pool_scale, w_branch=w_branch, w_out=w_out, norm2_g=norm2_g,
                  w_up=w_up, conv_w=conv_w, conv_b=conv_b, w_down=w_down)
    bp, tp, _ = x_prompt.shape
    bs, ts, _ = x_sample.shape
    depth = w_in.shape[0]
    n_pages = page_table.shape[1]
    past_len = n_pages * PAGE
    n_s = bs * ts
    tm_p = _tile(tp, 256)
    tm_s = _tile(n_s, 256)
    tf_p = _tile(tp, 512)
    sb_s = tm_s // ts
    pp = 16 if n_pages % 16 == 0 else n_pages
    gs = 4 if bs % 4 == 0 else 1
    consts = _make_consts(tp, ts, past_len, tm_s)
    zero_pool = jnp.zeros((bp, POOL_STATE, D_POOL), F32)
    zero_conv = jnp.zeros((bp, 2, D_FF), F32)

    xp = x_prompt.reshape(bp * tp, D_MODEL)
    xs = x_sample.reshape(n_s, D_MODEL)
    new_p = [[] for _ in range(5)]
    new_s = [[] for _ in range(5)]
    for l in range(depth):
        w = _layer_weights(l, params)
        lam_init = 0.8 - 0.6 * math.exp(-0.3 * l)
        zs, gl = _inproj(xp, w["g1"], w["ws"], w["wg"], tm_p)
        qa, a_new, qb, b_new, qc, _, c_new, kc, vc = _prep(
            zs, consts["rt_p"], w["gains"], consts["g64"], consts["g32"], consts["g96"],
            w["wuq"], w["wkc"], w["wuv"], tm_p)
        oa = _moba_prompt(qa, a_new, consts, bp, tp)
        ob = _diff_prompt(qb, b_new, w["b_lam"], w["g_sub"], consts, bp, tp, lam_init)
        oc = _mla_prompt(qc, kc, vc, consts, bp, tp)
        od, pool_new = _pool(zs, zero_pool, consts["cnt_p"], w["pw"], w["ps"], 1, tp)
        h = _merge(xp, oa, ob, oc, od, gl, w["wb"], w["wo"], tm_p)
        xp, conv_new = _ffn(h, zero_conv, w["g2"], w["wup"], w["wdn"], w["cw"], w["cb"],
                            1, tf_p, tp // tf_p)
        for k, v in enumerate((a_new.reshape(bp, tp, A_KV, 128), b_new.reshape(bp, tp, 1, 128),
                               c_new.reshape(bp, tp, 160), pool_new, conv_new)):
            new_p[k].append(v)
        zs, gl = _inproj(xs, w["g1"], w["ws"], w["wg"], tm_s)
        qa, a_new, qb, b_new, _, qce, c_new, _, _ = _prep(
            zs, consts["rt_s"], w["gains"], consts["g64"], consts["g32"], consts["g96"],
            w["wuq"], w["wkc"], w["wuv"], tm_s)
        oa = _moba_sample(qa, a_new, cache_a, l, page_table, consts, bs, ts, gs, pp)
        ob = _diff_sample(qb, b_new, w["b_lam"], w["g_sub"], cache_b, l, page_table, consts, bs, ts,
                          gs, pp, lam_init)
        oc = _mla_sample(qce, c_new, w["mq"], w["wext"], w["wuv"], cache_c, l, page_table, consts,
                         bs, ts, gs, pp)
        od, pool_new = _pool(zs, state_pool[l], consts["cnt_s"], w["pw"], w["ps"], sb_s, ts)
        h = _merge(xs, oa, ob, oc, od, gl, w["wb"], w["wo"], tm_s)
        xs, conv_new = _ffn(h, state_conv[l], w["g2"], w["wup"], w["wdn"], w["cw"], w["cb"],
                            sb_s, ts, 1)
        for k, v in enumerate((a_new.reshape(bs, ts, A_KV, 128), b_new.reshape(bs, ts, 1, 128),
                               c_new.reshape(bs, ts, 160), pool_new, conv_new)):
            new_s[k].append(v)
    np_ = [jnp.stack(v, axis=0) for v in new_p]
    ns_ = [jnp.stack(v, axis=0) for v in new_s]
    return (xp.reshape(bp, tp, D_MODEL), xs.reshape(bs, ts, D_MODEL),
            np_[0], np_[1], np_[2], np_[3], np_[4], ns_[0], ns_[1], ns_[2], ns_[3], ns_[4])
```
